```python
import jax, jax.numpy as jnp
from jax import lax
import numpy as np


D_MODEL = 2048
BATCH = 2
SEQ = 16384
DEPTH = 4

CHUNK = 64
D_MIX = D_MODEL
CONV_WIDTH = D_MIX // 2
CONV_KERNEL = 31
SSD_WIDTH = D_MIX - CONV_WIDTH
SSD_HEAD_DIM = 64
SSD_HEADS = SSD_WIDTH // SSD_HEAD_DIM
SSD_GROUPS = 2
SSD_HPG = SSD_HEADS // SSD_GROUPS
SSD_STATE = 128
SSD_CONV = 4
SSD_CHUNK = CHUNK
SSD_BC = SSD_GROUPS * SSD_STATE
D_IN = 2 * CONV_WIDTH + 2 * SSD_WIDTH + 2 * SSD_BC + SSD_HEADS
PEER_HEADS = 8
PEER_KEYS = 128
PEER_EXPERTS = PEER_KEYS * PEER_KEYS
PEER_TOPK = 16
PEER_QDIM = 256
PEER_HALF = PEER_QDIM // 2
PEER_BLOCK = 128
PLE_DIM = 256
DN_ALPHA = (2 * DEPTH) ** 0.25
DN_BETA = (8 * DEPTH) ** -0.25
LN_EPS = 1e-5
RMS_EPS = 1e-5

kernel_name = 'hybrid_conv_ssd_peer_deepnorm_trunk'


def layer_norm(x, g, b):
    xf = x.astype(jnp.float32)
    mu = jnp.mean(xf, axis=-1, keepdims=True)
    var = jnp.mean(jnp.square(xf - mu), axis=-1, keepdims=True)
    return ((xf - mu) * lax.rsqrt(var + LN_EPS)).astype(x.dtype) * g + b


def causal_depthwise_conv(x, w, b):
    k = w.shape[0]
    y = lax.conv_general_dilated(x, w[:, None, :], window_strides=(1,), padding=[(k - 1, 0)],
                                 dimension_numbers=('NWC', 'WIO', 'NWC'),
                                 feature_group_count=x.shape[-1])
    return y + b


def conformer_conv(a, gate, w, b, g, beta):
    u = a * jax.nn.sigmoid(gate)
    u = causal_depthwise_conv(u, w, b)
    return jax.nn.silu(layer_norm(u, g, beta))


def ssd_mixer(z, xbc, dt_raw, conv_w, conv_b, dt_bias, a_log, d_skip, norm_g):
    bsz, L, _ = z.shape
    dtype = z.dtype
    Q, G, H, P, N = SSD_CHUNK, SSD_GROUPS, SSD_HPG, SSD_HEAD_DIM, SSD_STATE
    nc = L // Q
    xbc = jax.nn.silu(causal_depthwise_conv(xbc, conv_w, conv_b))
    xs, Bm, Cm = jnp.split(xbc, [SSD_WIDTH, SSD_WIDTH + SSD_BC], axis=-1)
    xs = xs.reshape(bsz, nc, Q, G, H, P)
    Bm = Bm.reshape(bsz, nc, Q, G, N)
    Cm = Cm.reshape(bsz, nc, Q, G, N)
    dt = jax.nn.softplus((dt_raw + dt_bias).astype(jnp.float32)).reshape(bsz, nc, Q, G, H)
    A = -jnp.exp(a_log.astype(jnp.float32)).reshape(G, H)
    a_cum = jnp.cumsum(dt * A, axis=2)
    x_dt = xs * dt[..., None].astype(dtype)
    causal = jnp.tril(jnp.ones((Q, Q), dtype=bool))
    seg = a_cum[:, :, :, None] - a_cum[:, :, None, :]
    decay = jnp.exp(jnp.where(causal[:, :, None, None], seg, -jnp.inf)).astype(dtype)
    cb = jnp.einsum('bclgn,bcsgn->bclsg', Cm, Bm)
    y_diag = jnp.einsum('bclsg,bclsgh,bcsghp->bclghp', cb, decay, x_dt)
    decay_states = jnp.exp(a_cum[:, :, -1:] - a_cum).astype(dtype)
    states = jnp.einsum('bcsgn,bcsgh,bcsghp->bcghpn', Bm, decay_states, x_dt).astype(jnp.float32)
    chunk_decay = jnp.exp(a_cum[:, :, -1])

    def step(h, inp):
        s_c, d_c = inp
        return h * d_c[..., None, None] + s_c, h

    h0 = jnp.zeros((bsz, G, H, P, N), jnp.float32)
    _, prev = lax.scan(step, h0, (jnp.moveaxis(states, 1, 0), jnp.moveaxis(chunk_decay, 1, 0)))
    prev = jnp.moveaxis(prev, 0, 1).astype(dtype)
    y_off = jnp.einsum('bclgn,bcghpn,bclgh->bclghp', Cm, prev, jnp.exp(a_cum).astype(dtype))
    y = y_diag + y_off + xs * d_skip.reshape(G, H)[..., None]
    y = y.reshape(bsz, L, SSD_WIDTH)
    yg = (y * jax.nn.silu(z)).reshape(bsz, L, G, SSD_WIDTH // G).astype(jnp.float32)
    yg = yg * lax.rsqrt(jnp.mean(jnp.square(yg), axis=-1, keepdims=True) + RMS_EPS)
    return yg.reshape(bsz, L, SSD_WIDTH).astype(dtype) * norm_g


def peer(x, w_q, keys, u_tab, v_tab):
    bsz, L, D = x.shape
    T = bsz * L
    xt = x.reshape(T, D)
    q = (xt @ w_q).reshape(T, PEER_HEADS, 2, PEER_HALF).astype(jnp.float32)
    s = jnp.einsum('thid,hikd->thik', q, keys.astype(jnp.float32))
    top_s, top_i = lax.top_k(s, PEER_TOPK)
    cand = top_s[:, :, 0, :, None] + top_s[:, :, 1, None, :]
    best_s, best_j = lax.top_k(cand.reshape(T, PEER_HEADS, PEER_TOPK * PEER_TOPK), PEER_TOPK)
    i1 = jnp.take_along_axis(top_i[:, :, 0], best_j // PEER_TOPK, axis=-1)
    i2 = jnp.take_along_axis(top_i[:, :, 1], best_j % PEER_TOPK, axis=-1)
    nb = T // PEER_BLOCK
    idx = (i1 * PEER_KEYS + i2).reshape(nb, PEER_BLOCK, PEER_HEADS * PEER_TOPK)
    gate = jax.nn.softmax(best_s, axis=-1).astype(x.dtype).reshape(nb, PEER_BLOCK, PEER_HEADS * PEER_TOPK)
    xb = xt.reshape(nb, PEER_BLOCK, D)

    def block(args):
        xb_, ib, gb = args
        u = jnp.take(u_tab, ib, axis=0)
        act = jax.nn.gelu(jnp.einsum('tkd,td->tk', u, xb_), approximate=False)
        v = jnp.take(v_tab, ib, axis=0)
        return jnp.einsum('tk,tkd->td', gb * act, v)

    out = lax.map(block, (xb, idx, gate))
    return out.reshape(bsz, L, D)


def setup_inputs(seed: int = 0) -> dict:
    key = jax.random.key(seed)
    ks = jax.random.split(key, 32)
    f32 = jnp.float32
    nrm = lambda k, shape, scale: jax.random.normal(k, shape, f32) * scale
    x = nrm(ks[0], (BATCH, SEQ, D_MODEL), 1.0)
    p = nrm(ks[1], (DEPTH, BATCH, SEQ, PLE_DIM), 1.0)
    ln_in_g = 1.0 + nrm(ks[2], (D_MODEL,), 0.01)
    ln_in_b = nrm(ks[3], (D_MODEL,), 0.01)
    w_in = nrm(ks[4], (DEPTH, D_MODEL, D_IN), D_MODEL ** -0.5)
    conv_w = nrm(ks[5], (DEPTH, CONV_KERNEL, CONV_WIDTH), CONV_KERNEL ** -0.5)
    conv_b = nrm(ks[6], (DEPTH, CONV_WIDTH), 0.01)
    conv_ln_g = 1.0 + nrm(ks[7], (DEPTH, CONV_WIDTH), 0.01)
    conv_ln_b = nrm(ks[8], (DEPTH, CONV_WIDTH), 0.01)
    ssd_conv_w = nrm(ks[9], (DEPTH, SSD_CONV, SSD_WIDTH + 2 * SSD_BC), SSD_CONV ** -0.5)
    ssd_conv_b = nrm(ks[10], (DEPTH, SSD_WIDTH + 2 * SSD_BC), 0.01)
    dt0 = jnp.exp(jax.random.uniform(ks[11], (DEPTH, SSD_HEADS), f32, np.log(1e-3), np.log(1e-1)))
    dt_bias = dt0 + jnp.log(-jnp.expm1(-dt0))
    a_log = jnp.log(jax.random.uniform(ks[12], (DEPTH, SSD_HEADS), f32, 1.0, 16.0))
    d_skip = 1.0 + nrm(ks[13], (DEPTH, SSD_HEADS), 0.01)
    ssd_norm_g = 1.0 + nrm(ks[14], (DEPTH, SSD_WIDTH), 0.01)
    w_out = nrm(ks[15], (DEPTH, D_MIX, D_MODEL), DN_BETA * D_MIX ** -0.5)
    ln1_g = 1.0 + nrm(ks[16], (DEPTH, D_MODEL), 0.01)
    ln1_b = nrm(ks[17], (DEPTH, D_MODEL), 0.01)
    peer_wq = nrm(ks[18], (DEPTH, D_MODEL, PEER_HEADS * PEER_QDIM), D_MODEL ** -0.5)
    peer_keys = nrm(ks[19], (DEPTH, PEER_HEADS, 2, PEER_KEYS, PEER_HALF), PEER_HALF ** -0.5)
    peer_u = nrm(ks[20], (DEPTH, PEER_EXPERTS, D_MODEL), D_MODEL ** -0.5)
    peer_v = nrm(ks[21], (DEPTH, PEER_EXPERTS, D_MODEL), DN_BETA * PEER_HEADS ** -0.5)
    ple_w_gate = nrm(ks[22], (DEPTH, D_MODEL, D_MODEL), D_MODEL ** -0.5)
    ple_w_proj = nrm(ks[23], (DEPTH, PLE_DIM, D_MODEL), DN_BETA * PLE_DIM ** -0.5)
    ln2_g = 1.0 + nrm(ks[24], (DEPTH, D_MODEL), 0.01)
    ln2_b = nrm(ks[25], (DEPTH, D_MODEL), 0.01)
    return {'x': x, 'p': p, 'ln_in_g': ln_in_g, 'ln_in_b': ln_in_b, 'w_in': w_in,
            'conv_w': conv_w, 'conv_b': conv_b, 'conv_ln_g': conv_ln_g, 'conv_ln_b': conv_ln_b,
            'ssd_conv_w': ssd_conv_w, 'ssd_conv_b': ssd_conv_b, 'dt_bias': dt_bias, 'a_log': a_log,
            'd_skip': d_skip, 'ssd_norm_g': ssd_norm_g, 'w_out': w_out, 'ln1_g': ln1_g, 'ln1_b': ln1_b,
            'peer_wq': peer_wq, 'peer_keys': peer_keys, 'peer_u': peer_u, 'peer_v': peer_v,
            'ple_w_gate': ple_w_gate, 'ple_w_proj': ple_w_proj, 'ln2_g': ln2_g, 'ln2_b': ln2_b}


def reference(x, p, ln_in_g, ln_in_b, w_in, conv_w, conv_b, conv_ln_g, conv_ln_b,
              ssd_conv_w, ssd_conv_b, dt_bias, a_log, d_skip, ssd_norm_g, w_out, ln1_g, ln1_b,
              peer_wq, peer_keys, peer_u, peer_v, ple_w_gate, ple_w_proj, ln2_g, ln2_b):
    splits = [CONV_WIDTH, 2 * CONV_WIDTH, 2 * CONV_WIDTH + SSD_WIDTH,
              2 * CONV_WIDTH + 2 * SSD_WIDTH + 2 * SSD_BC]
    h = layer_norm(x, ln_in_g, ln_in_b)
    for i in range(DEPTH):
        proj = h @ w_in[i]
        glu_a, glu_g, z, xbc, dt_raw = jnp.split(proj, splits, axis=-1)
        y_conv = conformer_conv(glu_a, glu_g, conv_w[i], conv_b[i], conv_ln_g[i], conv_ln_b[i])
        y_ssd = ssd_mixer(z, xbc, dt_raw, ssd_conv_w[i], ssd_conv_b[i], dt_bias[i], a_log[i],
                          d_skip[i], ssd_norm_g[i])
        mix = jnp.concatenate([y_conv, y_ssd], axis=-1) @ w_out[i]
        h = layer_norm(DN_ALPHA * h + mix, ln1_g[i], ln1_b[i])
        ffn = peer(h, peer_wq[i], peer_keys[i], peer_u[i], peer_v[i])
        ple = jax.nn.sigmoid(h @ ple_w_gate[i]) * (p[i] @ ple_w_proj[i])
        h = layer_norm(DN_ALPHA * h + ffn + ple, ln2_g[i], ln2_b[i])
    return h
```

```python
import functools

import jax
import jax.numpy as jnp
from jax import lax
from jax.experimental import pallas as pl
from jax.experimental.pallas import tpu as pltpu

LN_EPS = 1e-5
RMS_EPS = 1e-5
SSD_GROUPS = 2
SSD_HEAD_DIM = 64
SSD_CHUNK = 64
PEER_TOPK = 16
LANES = 128
SUBLANES = 8
F32 = jnp.float32
BF16 = jnp.bfloat16
HIGHEST = lax.Precision.HIGHEST
NEG_INF = float("-inf")
MIB = 1024 * 1024


def _params(sem, vmem_mib):
    return pltpu.CompilerParams(dimension_semantics=sem, vmem_limit_bytes=vmem_mib * MIB)


def _layer_norm(v, g, b):
    mu = jnp.mean(v, axis=-1, keepdims=True)
    d = v - mu
    var = jnp.mean(d * d, axis=-1, keepdims=True)
    return d * lax.rsqrt(var + LN_EPS) * g + b


def _sigmoid(v):
    return 1.0 / (1.0 + jnp.exp(-v))


def _dot(a, b):
    return jnp.dot(a, b, preferred_element_type=F32)


def _dot_nt(a, b):
    return lax.dot_general(a, b, (((1,), (1,)), ((), ())), preferred_element_type=F32)


def _dot_exact(a, b):
    return jnp.dot(a, b, preferred_element_type=F32, precision=HIGHEST)


def _ln_kernel(x_ref, g_ref, b_ref, o_ref):
    o_ref[...] = _layer_norm(x_ref[...], g_ref[...], b_ref[...])


def _ln_call(x, g, b, tm):
    t, d = x.shape
    return pl.pallas_call(
        _ln_kernel,
        grid=(t // tm,),
        in_specs=[pl.BlockSpec((tm, d), lambda i: (i, 0)),
                  pl.BlockSpec((1, d), lambda i: (0, 0)),
                  pl.BlockSpec((1, d), lambda i: (0, 0))],
        out_specs=pl.BlockSpec((tm, d), lambda i: (i, 0)),
        out_shape=jax.ShapeDtypeStruct((t, d), F32),
        compiler_params=_params(("arbitrary",), 32),
        name="ln_in",
    )(x, g.reshape(1, d), b.reshape(1, d))


def _matmul_kernel(x_ref, w_ref, o_ref):
    o_ref[...] = _dot(x_ref[...].astype(BF16), w_ref[...])


def _matmul_call(x, w, tm, tn, name):
    m, k = x.shape
    n = w.shape[1]
    return pl.pallas_call(
        _matmul_kernel,
        grid=(n // tn, m // tm),
        in_specs=[pl.BlockSpec((tm, k), lambda j, i: (i, 0)),
                  pl.BlockSpec((k, tn), lambda j, i: (0, j))],
        out_specs=pl.BlockSpec((tm, tn), lambda j, i: (i, j)),
        out_shape=jax.ShapeDtypeStruct((m, n), F32),
        compiler_params=_params(("arbitrary", "arbitrary"), 48),
        name=name,
    )(x, w)


CONV_HALO = 32
CONV_ROWS = 16


def _conv_kernel(a_ref, g_ref, w_ref, b_ref, lg_ref, lb_ref, o_ref, ubuf, *, ts, taps):
    @pl.when(pl.program_id(1) == 0)
    def _():
        ubuf[0:CONV_HALO, :] = jnp.zeros((CONV_HALO, ubuf.shape[1]), F32)

    ubuf[CONV_HALO:CONV_HALO + ts, :] = a_ref[...] * _sigmoid(g_ref[...])
    bias = b_ref[...]
    lg = lg_ref[...]
    lb = lb_ref[...]
    for r0 in range(0, ts, CONV_ROWS):
        acc = jnp.zeros((CONV_ROWS, ubuf.shape[1]), F32) + bias
        for k in range(taps):
            off = CONV_HALO - (taps - 1) + k + r0
            acc = acc + ubuf[off:off + CONV_ROWS, :] * w_ref[k:k + 1, :]
        y = _layer_norm(acc, lg, lb)
        o_ref[r0:r0 + CONV_ROWS, :] = y * _sigmoid(y)
    ubuf[0:CONV_HALO, :] = ubuf[ts:ts + CONV_HALO, :]


def _conv_call(proj, w, b, lg, lb, batch, seq, ts):
    taps, c = w.shape
    nt = seq // ts
    t = batch * seq
    cb = c // c
    del cb
    return pl.pallas_call(
        functools.partial(_conv_kernel, ts=ts, taps=taps),
        grid=(batch, nt),
        in_specs=[pl.BlockSpec((ts, c), lambda bi, j: (bi * nt + j, 0)),
                  pl.BlockSpec((ts, c), lambda bi, j: (bi * nt + j, 1)),
                  pl.BlockSpec((taps, c), lambda bi, j: (0, 0)),
                  pl.BlockSpec((1, c), lambda bi, j: (0, 0)),
                  pl.BlockSpec((1, c), lambda bi, j: (0, 0)),
                  pl.BlockSpec((1, c), lambda bi, j: (0, 0))],
        out_specs=pl.BlockSpec((ts, c), lambda bi, j: (bi * nt + j, 0)),
        out_shape=jax.ShapeDtypeStruct((t, c), F32),
        scratch_shapes=[pltpu.VMEM((CONV_HALO + ts, c), F32)],
        compiler_params=_params(("arbitrary", "arbitrary"), 32),
        name="conv_module",
    )(proj, proj, w, b.reshape(1, c), lg.reshape(1, c), lb.reshape(1, c))


SSD_HALO = 8


def _softplus(v):
    return jnp.maximum(v, 0.0) + jnp.log(1.0 + jnp.exp(-jnp.abs(v)))


def _ssd_kernel(z_ref, xbc_ref, dt_ref, dtp_ref, cw_ref, cb_ref, dtb_ref, alog_ref, dtbp_ref,
                alogp_ref, dskip_ref, ng_ref, exp_ref, tri_ref, bd2_ref, o_ref,
                xbuf, prev, ybuf, *, ts, taps, width, nstate):
    q = SSD_CHUNK
    gw = width // SSD_GROUPS
    pairs_per_group = gw // LANES

    @pl.when(pl.program_id(1) == 0)
    def _():
        xbuf[0:SSD_HALO, :] = jnp.zeros((SSD_HALO, xbuf.shape[1]), F32)
        prev[...] = jnp.zeros(prev.shape, F32)

    xbuf[SSD_HALO:SSD_HALO + ts, :] = xbc_ref[...]

    a_neg = -jnp.exp(alog_ref[...])
    dt = _softplus(dt_ref[...] + dtb_ref[...])
    expand = exp_ref[...]
    dtx = _dot_exact(dt, expand)
    dax = _dot_exact(dt * a_neg, expand)
    a_neg_p = -jnp.exp(alogp_ref[...])
    tri = tri_ref[...]
    bd2 = bd2_ref[...]
    lane = lax.broadcasted_iota(jnp.int32, (q, LANES), 1)
    row = lax.broadcasted_iota(jnp.int32, (q, LANES), 0)
    causal2 = row >= (lane % q)
    left = lane < q

    for c in range(ts // q):
        r0 = c * q
        xc = jnp.zeros((q, xbuf.shape[1]), F32) + cb_ref[...]
        for k in range(taps):
            off = SSD_HALO - (taps - 1) + k + r0
            xc = xc + xbuf[off:off + q, :] * cw_ref[k:k + 1, :]
        xc = xc * _sigmoid(xc)
        xs = xc[:, :width]
        bm = xc[:, width:width + SSD_GROUPS * nstate]
        cm = xc[:, width + SSD_GROUPS * nstate:]

        acx = _dot_exact(tri, dax[r0:r0 + q, :])
        dap = _softplus(dtp_ref[:, c * LANES:(c + 1) * LANES] + dtbp_ref[...]) * a_neg_p
        arow = _dot_exact(dap, bd2)
        last = acx[q - 1:q, :]
        x_dt = xs * dtx[r0:r0 + q, :]
        x_state = (x_dt * jnp.exp(last - acx)).astype(BF16)
        x_dt_b = x_dt.astype(BF16)
        exp_ac = jnp.exp(acx)
        chunk_decay = jnp.exp(last)
        skip = xs * dskip_ref[...]

        for g in range(SSD_GROUPS):
            gs = slice(g * gw, (g + 1) * gw)
            bg = bm[:, g * nstate:(g + 1) * nstate].astype(BF16)
            cg = cm[:, g * nstate:(g + 1) * nstate].astype(BF16)
            cb2 = _dot_nt(cg, jnp.concatenate([bg, bg], axis=0))
            prev_g = prev[g]
            y_off = _dot(cg, prev_g.astype(BF16)) * exp_ac[:, gs]
            states = lax.dot_general(bg, x_state[:, gs], (((0,), (0,)), ((), ())),
                                     preferred_element_type=F32)
            prev[g] = prev_g * chunk_decay[:, gs] + states
            for pr in range(pairs_per_group):
                pi = g * pairs_per_group + pr
                cols = slice(pi * LANES, (pi + 1) * LANES)
                seg = acx[:, cols] - arow[pi:pi + 1, :]
                decay = jnp.exp(jnp.where(causal2, seg, NEG_INF))
                m2 = (cb2 * decay).astype(BF16)
                xp = x_dt_b[:, cols]
                zero = jnp.zeros_like(xp)
                bdx = jnp.concatenate([jnp.where(left, xp, zero), jnp.where(left, zero, xp)], axis=0)
                y_diag = _dot(m2, bdx)
                ybuf[r0:r0 + q, cols] = (y_diag + y_off[:, pr * LANES:(pr + 1) * LANES] + skip[:, cols])

    xbuf[0:SSD_HALO, :] = xbuf[ts:ts + SSD_HALO, :]

    z = z_ref[...]
    yg = ybuf[...] * (z * _sigmoid(z))
    ng = ng_ref[...]
    for g in range(SSD_GROUPS):
        gs = slice(g * gw, (g + 1) * gw)
        v = yg[:, gs]
        ms = jnp.mean(v * v, axis=-1, keepdims=True)
        o_ref[:, gs] = v * lax.rsqrt(ms + RMS_EPS) * ng[:, gs]


def _ssd_call(proj, dt_raw, dtp, cw, cb, dt_bias, a_log, d_skip, norm_g, batch, seq, ts, conv_width):
    taps, cx = cw.shape
    width = norm_g.shape[0]
    heads = dt_bias.shape[0]
    nstate = (cx - width) // (2 * SSD_GROUPS)
    npairs = heads // 2
    q = SSD_CHUNK
    nt = seq // ts
    t = batch * seq
    z_blk = (2 * conv_width) // width
    x_blk = (2 * conv_width + width) // cx
    assert z_blk * width == 2 * conv_width and x_blk * cx == 2 * conv_width + width
    pad = LANES - heads
    dtb = jnp.pad(dt_bias, (0, pad)).reshape(1, LANES)
    alog = jnp.pad(a_log, (0, pad)).reshape(1, LANES)
    dtbp = jnp.repeat(dt_bias.reshape(npairs, 2), q, axis=1)
    alogp = jnp.repeat(a_log.reshape(npairs, 2), q, axis=1)
    dskip = jnp.repeat(d_skip, SSD_HEAD_DIM).reshape(1, width)
    head_of_col = jnp.arange(width) // SSD_HEAD_DIM
    expand = (jnp.arange(LANES)[:, None] == head_of_col[None, :]).astype(F32)
    ii = jnp.arange(q)
    tri = (ii[:, None] >= ii[None, :]).astype(F32)
    jj = jnp.arange(LANES)
    bd2 = ((jj[:, None] // q == jj[None, :] // q) & (jj[:, None] % q <= jj[None, :] % q)).astype(F32)
    const = lambda bi, j: (0, 0)
    return pl.pallas_call(
        functools.partial(_ssd_kernel, ts=ts, taps=taps, width=width, nstate=nstate),
        grid=(batch, nt),
        in_specs=[pl.BlockSpec((ts, width), lambda bi, j: (bi * nt + j, z_blk)),
                  pl.BlockSpec((ts, cx), lambda bi, j: (bi * nt + j, x_blk)),
                  pl.BlockSpec((ts, LANES), lambda bi, j: (bi * nt + j, 0)),
                  pl.BlockSpec((npairs, 2 * ts), lambda bi, j: (0, bi * nt + j)),
                  pl.BlockSpec((taps, cx), const),
                  pl.BlockSpec((1, cx), const),
                  pl.BlockSpec((1, LANES), const),
                  pl.BlockSpec((1, LANES), const),
                  pl.BlockSpec((npairs, LANES), const),
                  pl.BlockSpec((npairs, LANES), const),
                  pl.BlockSpec((1, width), const),
                  pl.BlockSpec((1, width), const),
                  pl.BlockSpec((LANES, width), const),
                  pl.BlockSpec((q, q), const),
                  pl.BlockSpec((LANES, LANES), const)],
        out_specs=pl.BlockSpec((ts, width), lambda bi, j: (bi * nt + j, 0)),
        out_shape=jax.ShapeDtypeStruct((t, width), F32),
        scratch_shapes=[pltpu.VMEM((SSD_HALO + ts, cx), F32),
                        pltpu.VMEM((SSD_GROUPS, nstate, width // SSD_GROUPS), F32),
                        pltpu.VMEM((ts, width), F32)],
        compiler_params=_params(("arbitrary", "arbitrary"), 48),
        name="ssd_mixer",
    )(proj, proj, dt_raw, dtp, cw, cb.reshape(1, cx), dtb, alog, dtbp, alogp, dskip,
      norm_g.reshape(1, width), expand, tri, bd2)


def _outproj_kernel(yc_ref, ys_ref, h_ref, w1_ref, w2_ref, g_ref, b_ref, o_ref, *, alpha):
    mix = _dot(yc_ref[...].astype(BF16), w1_ref[...]) + _dot(ys_ref[...].astype(BF16), w2_ref[...])
    o_ref[...] = _layer_norm(alpha * h_ref[...] + mix, g_ref[...], b_ref[...])


def _outproj_call(y_conv, y_ssd, h, w1, w2, g, b, alpha, tm):
    t, d = h.shape
    c1 = y_conv.shape[1]
    c2 = y_ssd.shape[1]
    const = lambda i: (0, 0)
    return pl.pallas_call(
        functools.partial(_outproj_kernel, alpha=alpha),
        grid=(t // tm,),
        in_specs=[pl.BlockSpec((tm, c1), lambda i: (i, 0)),
                  pl.BlockSpec((tm, c2), lambda i: (i, 0)),
                  pl.BlockSpec((tm, d), lambda i: (i, 0)),
                  pl.BlockSpec((c1, d), const),
                  pl.BlockSpec((c2, d), const),
                  pl.BlockSpec((1, d), const),
                  pl.BlockSpec((1, d), const)],
        out_specs=pl.BlockSpec((tm, d), lambda i: (i, 0)),
        out_shape=jax.ShapeDtypeStruct((t, d), F32),
        compiler_params=_params(("arbitrary",), 48),
        name="outproj_ln",
    )(y_conv, y_ssd, h, w1, w2, g.reshape(1, d), b.reshape(1, d))


def _topk_rows(s, iota, nvals, k, store):
    for r in range(k):
        m = jnp.max(s, axis=0, keepdims=True)
        idx = jnp.min(jnp.where(s == m, iota, nvals), axis=0, keepdims=True)
        store(r, m, idx)
        s = jnp.where(iota == idx, NEG_INF, s)


def _route_kernel(h_ref, wq_ref, keys_ref, idx_ref, gate_ref, q_scr, ts_scr, ti_scr, bs_scr, bj_scr,
                  *, tm, heads, nkeys, half):
    k = PEER_TOPK
    q_scr[...] = _dot(h_ref[...].astype(BF16), wq_ref[...])
    iota1 = lax.broadcasted_iota(jnp.int32, (nkeys, tm), 0)
    iota2 = lax.broadcasted_iota(jnp.int32, (k * k, LANES), 0)
    for hd in range(heads):
        for side in range(2):
            hs = hd * 2 + side
            qh = q_scr[:, hs * half:(hs + 1) * half].astype(BF16)
            s = _dot_nt(keys_ref[hs], qh)

            def store1(r, m, idx, side=side):
                ts_scr[side, r:r + 1, :] = m
                ti_scr[side, r:r + 1, :] = idx

            _topk_rows(s, iota1, nkeys, k, store1)
        for lg in range(tm // LANES):
            ls = slice(lg * LANES, (lg + 1) * LANES)
            s0 = ts_scr[0, :, ls]
            s1 = ts_scr[1, :, ls]
            cand = jnp.concatenate([s0[k1:k1 + 1, :] + s1 for k1 in range(k)], axis=0)

            def store2(r, m, idx):
                bs_scr[r:r + 1, :] = m
                bj_scr[r:r + 1, :] = idx

            _topk_rows(cand, iota2, k * k, k, store2)
            best_s = bs_scr[...]
            best_j = bj_scr[...]
            k1 = best_j // k
            k2 = best_j % k
            i0 = ti_scr[0, :, ls]
            i1 = ti_scr[1, :, ls]
            e1 = jnp.zeros((k, LANES), jnp.int32)
            e2 = jnp.zeros((k, LANES), jnp.int32)
            for kk in range(k):
                e1 = e1 + jnp.where(k1 == kk, i0[kk:kk + 1, :], 0)
                e2 = e2 + jnp.where(k2 == kk, i1[kk:kk + 1, :], 0)
            ex = jnp.exp(best_s - best_s[0:1, :])
            gate = ex / jnp.sum(ex, axis=0, keepdims=True)
            idx_ref[hd * k:(hd + 1) * k, ls] = e1 * nkeys + e2
            gate_ref[hd * k:(hd + 1) * k, ls] = gate


def _route_call(h, wq, keys, tm):
    t, d = h.shape
    heads, _, nkeys, half = keys.shape
    qd = wq.shape[1]
    k = PEER_TOPK
    keys2 = keys.reshape(heads * 2, nkeys, half).astype(BF16)
    return pl.pallas_call(
        functools.partial(_route_kernel, tm=tm, heads=heads, nkeys=nkeys, half=half),
        grid=(t // tm,),
        in_specs=[pl.BlockSpec((tm, d), lambda i: (i, 0)),
                  pl.BlockSpec((d, qd), lambda i: (0, 0)),
                  pl.BlockSpec((heads * 2, nkeys, half), lambda i: (0, 0, 0))],
        out_specs=[pl.BlockSpec((heads * k, tm), lambda i: (0, i)),
                   pl.BlockSpec((heads * k, tm), lambda i: (0, i))],
        out_shape=[jax.ShapeDtypeStruct((heads * k, t), jnp.int32),
                   jax.ShapeDtypeStruct((heads * k, t), F32)],
        scratch_shapes=[pltpu.VMEM((tm, qd), F32),
                        pltpu.VMEM((2, k, tm), F32),
                        pltpu.VMEM((2, k, tm), jnp.int32),
                        pltpu.VMEM((k, LANES), F32),
                        pltpu.VMEM((k, LANES), jnp.int32)],
        compiler_params=_params(("arbitrary",), 48),
        name="peer_route",
    )(h, wq, keys2)


def _gelu(v):
    return 0.5 * v * (1.0 + lax.erf(v * (2.0 ** -0.5)))


def _gather_kernel(idx_cur_ref, idx_nxt_ref, x_ref, g_ref, u_hbm, v_hbm, o_ref, ubuf, vbuf, sem,
                   *, tb, ne):
    i = pl.program_id(0)
    n = pl.num_programs(0)
    slot = i % 2

    def issue(idx_ref, dst_slot):
        def body(t, carry):
            for k in range(ne):
                e = idx_ref[t, k]
                r = t * ne + k
                pltpu.make_async_copy(u_hbm.at[pl.ds(e, 1)], ubuf.at[dst_slot, pl.ds(r, 1)],
                                      sem.at[0, dst_slot]).start()
                pltpu.make_async_copy(v_hbm.at[pl.ds(e, 1)], vbuf.at[dst_slot, pl.ds(r, 1)],
                                      sem.at[1, dst_slot]).start()
            return carry
        lax.fori_loop(0, tb, body, 0)

    @pl.when(i == 0)
    def _():
        issue(idx_cur_ref, 0)

    @pl.when(i + 1 < n)
    def _():
        issue(idx_nxt_ref, 1 - slot)

    pltpu.make_async_copy(ubuf.at[slot], ubuf.at[slot], sem.at[0, slot]).wait()
    pltpu.make_async_copy(vbuf.at[slot], vbuf.at[slot], sem.at[1, slot]).wait()

    gates = g_ref[0]
    for t in range(tb):
        rows = pl.ds(t * ne, ne)
        x = x_ref[t:t + 1, :]
        act = jnp.sum(ubuf[slot, rows, :] * x, axis=-1, keepdims=True)
        w = gates[:, t:t + 1] * _gelu(act)
        o_ref[t:t + 1, :] = jnp.sum(vbuf[slot, rows, :] * w, axis=0, keepdims=True)


def _gather_call(h, idx, gate3, u_tab, v_tab, tb):
    t, d = h.shape
    ne = idx.shape[1]
    nb = t // tb
    return pl.pallas_call(
        functools.partial(_gather_kernel, tb=tb, ne=ne),
        grid=(nb,),
        in_specs=[pl.BlockSpec((tb, ne), lambda i: (i, 0), memory_space=pltpu.SMEM),
                  pl.BlockSpec((tb, ne), lambda i: (jnp.minimum(i + 1, nb - 1), 0),
                               memory_space=pltpu.SMEM),
                  pl.BlockSpec((tb, d), lambda i: (i, 0)),
                  pl.BlockSpec((1, ne, tb), lambda i: (i, 0, 0)),
                  pl.BlockSpec(memory_space=pl.ANY),
                  pl.BlockSpec(memory_space=pl.ANY)],
        out_specs=pl.BlockSpec((tb, d), lambda i: (i, 0)),
        out_shape=jax.ShapeDtypeStruct((t, d), F32),
        scratch_shapes=[pltpu.VMEM((2, tb * ne, d), F32),
                        pltpu.VMEM((2, tb * ne, d), F32),
                        pltpu.SemaphoreType.DMA((2, 2))],
        compiler_params=_params(("arbitrary",), 48),
        name="peer_gather",
    )(idx, idx, h, gate3, u_tab, v_tab)


def _ple_kernel(h_ref, f_ref, p_ref, wg_ref, wp_ref, g_ref, b_ref, o_ref, *, alpha):
    h = h_ref[...]
    gate = _sigmoid(_dot(h.astype(BF16), wg_ref[...]))
    emb = _dot(p_ref[...].astype(BF16), wp_ref[...])
    o_ref[...] = _layer_norm(alpha * h + f_ref[...] + gate * emb, g_ref[...], b_ref[...])


def _ple_call(h, ffn, p, wg, wp, g, b, alpha, tm):
    t, d = h.shape
    pd = p.shape[1]
    const = lambda i: (0, 0)
    return pl.pallas_call(
        functools.partial(_ple_kernel, alpha=alpha),
        grid=(t // tm,),
        in_specs=[pl.BlockSpec((tm, d), lambda i: (i, 0)),
                  pl.BlockSpec((tm, d), lambda i: (i, 0)),
                  pl.BlockSpec((tm, pd), lambda i: (i, 0)),
                  pl.BlockSpec((d, d), const),
                  pl.BlockSpec((pd, d), const),
                  pl.BlockSpec((1, d), const),
                  pl.BlockSpec((1, d), const)],
        out_specs=pl.BlockSpec((tm, d), lambda i: (i, 0)),
        out_shape=jax.ShapeDtypeStruct((t, d), F32),
        compiler_params=_params(("arbitrary",), 48),
        name="ple_ln",
    )(h, ffn, p, wg, wp, g.reshape(1, d), b.reshape(1, d))


def _tile(n, want):
    return want if n % want == 0 else n


def kernel(x, p, ln_in_g, ln_in_b, w_in, conv_w, conv_b, conv_ln_g, conv_ln_b, ssd_conv_w, ssd_conv_b, dt_bias, a_log, d_skip, ssd_norm_g, w_out, ln1_g, ln1_b, peer_wq, peer_keys, peer_u, peer_v, ple_w_gate, ple_w_proj, ln2_g, ln2_b):
    batch, seq, d = x.shape
    depth = w_in.shape[0]
    t = batch * seq
    conv_width = conv_w.shape[2]
    width = ssd_norm_g.shape[1]
    cx = ssd_conv_w.shape[2]
    heads = dt_bias.shape[1]
    n_main = 2 * conv_width + width + cx
    alpha = float((2 * depth) ** 0.25)
    q = SSD_CHUNK

    tm = _tile(t, 256)
    ts = _tile(seq, 256)
    tb = 8
    tn_main = n_main // 3 if (n_main // 3) % LANES == 0 and n_main % 3 == 0 else n_main

    h = _ln_call(x.reshape(t, d), ln_in_g, ln_in_b, tm)
    p2 = p.reshape(depth, t, p.shape[-1])
    for i in range(depth):
        w_main = w_in[i, :, :n_main].astype(BF16)
        w_dt = jnp.pad(w_in[i, :, n_main:], ((0, 0), (0, LANES - heads))).astype(BF16)
        proj = _matmul_call(h, w_main, _tile(t, 512), tn_main, "in_proj")
        dt_raw = _matmul_call(h, w_dt, _tile(t, 512), LANES, "dt_proj")
        dtp = (dt_raw[:, :heads].reshape(t // q, q, heads // 2, 2)
               .transpose(2, 0, 3, 1).reshape(heads // 2, 2 * t))
        y_conv = _conv_call(proj, conv_w[i], conv_b[i], conv_ln_g[i], conv_ln_b[i], batch, seq, ts)
        y_ssd = _ssd_call(proj, dt_raw, dtp, ssd_conv_w[i], ssd_conv_b[i], dt_bias[i], a_log[i],
                          d_skip[i], ssd_norm_g[i], batch, seq, ts, conv_width)
        w_o = w_out[i].astype(BF16)
        h = _outproj_call(y_conv, y_ssd, h, w_o[:conv_width], w_o[conv_width:], ln1_g[i], ln1_b[i],
                          alpha, tm)
        idx_t, gate_t = _route_call(h, peer_wq[i].astype(BF16), peer_keys[i], tm)
        ne = idx_t.shape[0]
        idx = idx_t.T
        gate3 = gate_t.reshape(ne, t // tb, tb).transpose(1, 0, 2)
        ffn = _gather_call(h, idx, gate3, peer_u[i], peer_v[i], tb)
        h = _ple_call(h, ffn, p2[i], ple_w_gate[i].astype(BF16), ple_w_proj[i].astype(BF16),
                      ln2_g[i], ln2_b[i], alpha, tm)
    return h.reshape(batch, seq, d)
```

```python
import functools

import jax
import jax.numpy as jnp
from jax import lax
from jax.experimental import pallas as pl
from jax.experimental.pallas import tpu as pltpu

LN_EPS = 1e-5
RMS_EPS = 1e-5
SSD_GROUPS = 2
SSD_HEAD_DIM = 64
SSD_CHUNK = 64
PEER_TOPK = 16
LANES = 128
SUBLANES = 8
F32 = jnp.float32
BF16 = jnp.bfloat16
HIGHEST = lax.Precision.HIGHEST
NEG_INF = float("-inf")
MIB = 1024 * 1024


def _params(sem, vmem_mib):
    return pltpu.CompilerParams(dimension_semantics=sem, vmem_limit_bytes=vmem_mib * MIB)


def _layer_norm(v, g, b):
    mu = jnp.mean(v, axis=-1, keepdims=True)
    d = v - mu
    var = jnp.mean(d * d, axis=-1, keepdims=True)
    return d * lax.rsqrt(var + LN_EPS) * g + b


def _sigmoid(v):
    return 1.0 / (1.0 + jnp.exp(-v))


def _dot(a, b):
    return jnp.dot(a, b, preferred_element_type=F32)


def _dot_nt(a, b):
    return lax.dot_general(a, b, (((1,), (1,)), ((), ())), preferred_element_type=F32)


def _dot_exact(a, b):
    return jnp.dot(a, b, preferred_element_type=F32, precision=HIGHEST)


def _ln_kernel(x_ref, g_ref, b_ref, o_ref):
    o_ref[...] = _layer_norm(x_ref[...], g_ref[...], b_ref[...])


def _ln_call(x, g, b, tm):
    t, d = x.shape
    return pl.pallas_call(
        _ln_kernel,
        grid=(t // tm,),
        in_specs=[pl.BlockSpec((tm, d), lambda i: (i, 0)),
                  pl.BlockSpec((1, d), lambda i: (0, 0)),
                  pl.BlockSpec((1, d), lambda i: (0, 0))],
        out_specs=pl.BlockSpec((tm, d), lambda i: (i, 0)),
        out_shape=jax.ShapeDtypeStruct((t, d), F32),
        compiler_params=_params(("arbitrary",), 32),
        name="ln_in",
    )(x, g.reshape(1, d), b.reshape(1, d))


def _matmul_kernel(x_ref, w_ref, o_ref):
    o_ref[...] = _dot(x_ref[...].astype(BF16), w_ref[...])


def _matmul_call(x, w, tm, tn, name):
    m, k = x.shape
    n = w.shape[1]
    return pl.pallas_call(
        _matmul_kernel,
        grid=(n // tn, m // tm),
        in_specs=[pl.BlockSpec((tm, k), lambda j, i: (i, 0)),
                  pl.BlockSpec((k, tn), lambda j, i: (0, j))],
        out_specs=pl.BlockSpec((tm, tn), lambda j, i: (i, j)),
        out_shape=jax.ShapeDtypeStruct((m, n), F32),
        compiler_params=_params(("arbitrary", "arbitrary"), 48),
        name=name,
    )(x, w)


CONV_HALO = 32
CONV_ROWS = 16


def _conv_kernel(a_ref, g_ref, w_ref, b_ref, lg_ref, lb_ref, o_ref, ubuf, shifted, *, ts, taps):
    rows = CONV_HALO + ts

    @pl.when(pl.program_id(1) == 0)
    def _():
        ubuf[0:CONV_HALO, :] = jnp.zeros((CONV_HALO, ubuf.shape[1]), F32)
        ubuf[rows:rows + SUBLANES, :] = jnp.zeros((SUBLANES, ubuf.shape[1]), F32)

    ubuf[CONV_HALO:rows, :] = a_ref[...] * _sigmoid(g_ref[...])
    for s in range(1, SUBLANES):
        shifted[s - 1] = ubuf[s:s + rows, :]
    bias = b_ref[...]
    lg = lg_ref[...]
    lb = lb_ref[...]
    for r0 in range(0, ts, CONV_ROWS):
        acc = jnp.zeros((CONV_ROWS, ubuf.shape[1]), F32) + bias
        for k in range(taps):
            off = CONV_HALO - (taps - 1) + k + r0
            s = off % SUBLANES
            base = off - s
            if s == 0:
                win = ubuf[base:base + CONV_ROWS, :]
            else:
                win = shifted[s - 1, base:base + CONV_ROWS, :]
            acc = acc + win * w_ref[k:k + 1, :]
        y = _layer_norm(acc, lg, lb)
        o_ref[r0:r0 + CONV_ROWS, :] = y * _sigmoid(y)
    ubuf[0:CONV_HALO, :] = ubuf[ts:ts + CONV_HALO, :]


def _conv_call(proj, w, b, lg, lb, batch, seq, ts):
    taps, c = w.shape
    nt = seq // ts
    t = batch * seq
    return pl.pallas_call(
        functools.partial(_conv_kernel, ts=ts, taps=taps),
        grid=(batch, nt),
        in_specs=[pl.BlockSpec((ts, c), lambda bi, j: (bi * nt + j, 0)),
                  pl.BlockSpec((ts, c), lambda bi, j: (bi * nt + j, 1)),
                  pl.BlockSpec((taps, c), lambda bi, j: (0, 0)),
                  pl.BlockSpec((1, c), lambda bi, j: (0, 0)),
                  pl.BlockSpec((1, c), lambda bi, j: (0, 0)),
                  pl.BlockSpec((1, c), lambda bi, j: (0, 0))],
        out_specs=pl.BlockSpec((ts, c), lambda bi, j: (bi * nt + j, 0)),
        out_shape=jax.ShapeDtypeStruct((t, c), F32),
        scratch_shapes=[pltpu.VMEM((CONV_HALO + ts + SUBLANES, c), F32),
                        pltpu.VMEM((SUBLANES - 1, CONV_HALO + ts, c), F32)],
        compiler_params=_params(("arbitrary", "arbitrary"), 48),
        name="conv_module",
    )(proj, proj, w, b.reshape(1, c), lg.reshape(1, c), lb.reshape(1, c))


SSD_HALO = 8


def _softplus(v):
    return jnp.maximum(v, 0.0) + jnp.log(1.0 + jnp.exp(-jnp.abs(v)))


def _ssd_kernel(z_ref, xbc_ref, dt_ref, dtp_ref, cw_ref, cb_ref, dtb_ref, alog_ref, dtbp_ref,
                alogp_ref, dskip_ref, ng_ref, exp_ref, tri_ref, bd2_ref, o_ref,
                xbuf, prev, ybuf, *, ts, taps, width, nstate):
    q = SSD_CHUNK
    gw = width // SSD_GROUPS
    pairs_per_group = gw // LANES

    @pl.when(pl.program_id(1) == 0)
    def _():
        xbuf[0:SSD_HALO, :] = jnp.zeros((SSD_HALO, xbuf.shape[1]), F32)
        prev[...] = jnp.zeros(prev.shape, F32)

    xbuf[SSD_HALO:SSD_HALO + ts, :] = xbc_ref[...]

    a_neg = -jnp.exp(alog_ref[...])
    dt = _softplus(dt_ref[...] + dtb_ref[...])
    expand = exp_ref[...]
    dtx = _dot_exact(dt, expand)
    dax = _dot_exact(dt * a_neg, expand)
    a_neg_p = -jnp.exp(alogp_ref[...])
    tri = tri_ref[...]
    bd2 = bd2_ref[...]
    lane = lax.broadcasted_iota(jnp.int32, (q, LANES), 1)
    row = lax.broadcasted_iota(jnp.int32, (q, LANES), 0)
    causal2 = row >= (lane % q)
    left = lane < q

    for c in range(ts // q):
        r0 = c * q
        xc = jnp.zeros((q, xbuf.shape[1]), F32) + cb_ref[...]
        for k in range(taps):
            off = SSD_HALO - (taps - 1) + k + r0
            xc = xc + xbuf[off:off + q, :] * cw_ref[k:k + 1, :]
        xc = xc * _sigmoid(xc)
        xs = xc[:, :width]
        bm = xc[:, width:width + SSD_GROUPS * nstate]
        cm = xc[:, width + SSD_GROUPS * nstate:]

        acx = _dot_exact(tri, dax[r0:r0 + q, :])
        dap = _softplus(dtp_ref[:, c * LANES:(c + 1) * LANES] + dtbp_ref[...]) * a_neg_p
        arow = _dot_exact(dap, bd2)
        last = acx[q - 1:q, :]
        x_dt = xs * dtx[r0:r0 + q, :]
        x_state = (x_dt * jnp.exp(last - acx)).astype(BF16)
        x_dt_b = x_dt.astype(BF16)
        exp_ac = jnp.exp(acx)
        chunk_decay = jnp.exp(last)
        skip = xs * dskip_ref[...]

        for g in range(SSD_GROUPS):
            gs = slice(g * gw, (g + 1) * gw)
            bg = bm[:, g * nstate:(g + 1) * nstate].astype(BF16)
            cg = cm[:, g * nstate:(g + 1) * nstate].astype(BF16)
            cb2 = _dot_nt(cg, jnp.concatenate([bg, bg], axis=0))
            prev_g = prev[g]
            y_off = _dot(cg, prev_g.astype(BF16)) * exp_ac[:, gs]
            states = lax.dot_general(bg, x_state[:, gs], (((0,), (0,)), ((), ())),
                                     preferred_element_type=F32)
            prev[g] = prev_g * chunk_decay[:, gs] + states
            for pr in range(pairs_per_group):
                pi = g * pairs_per_group + pr
                cols = slice(pi * LANES, (pi + 1) * LANES)
                seg = acx[:, cols] - arow[pi:pi + 1, :]
                decay = jnp.exp(jnp.where(causal2, seg, NEG_INF))
                m2 = (cb2 * decay).astype(BF16)
                xp = x_dt_b[:, cols]
                zero = jnp.zeros_like(xp)
                bdx = jnp.concatenate([jnp.where(left, xp, zero), jnp.where(left, zero, xp)], axis=0)
                y_diag = _dot(m2, bdx)
                ybuf[r0:r0 + q, cols] = (y_diag + y_off[:, pr * LANES:(pr + 1) * LANES] + skip[:, cols])

    xbuf[0:SSD_HALO, :] = xbuf[ts:ts + SSD_HALO, :]

    z = z_ref[...]
    yg = ybuf[...] * (z * _sigmoid(z))
    ng = ng_ref[...]
    for g in range(SSD_GROUPS):
        gs = slice(g * gw, (g + 1) * gw)
        v = yg[:, gs]
        ms = jnp.mean(v * v, axis=-1, keepdims=True)
        o_ref[:, gs] = v * lax.rsqrt(ms + RMS_EPS) * ng[:, gs]


def _ssd_call(proj, dt_raw, dtp, cw, cb, dt_bias, a_log, d_skip, norm_g, batch, seq, ts, conv_width):
    taps, cx = cw.shape
    width = norm_g.shape[0]
    heads = dt_bias.shape[0]
    nstate = (cx - width) // (2 * SSD_GROUPS)
    npairs = heads // 2
    q = SSD_CHUNK
    nt = seq // ts
    t = batch * seq
    z_blk = (2 * conv_width) // width
    x_blk = (2 * conv_width + width) // cx
    assert z_blk * width == 2 * conv_width and x_blk * cx == 2 * conv_width + width
    pad = LANES - heads
    dtb = jnp.pad(dt_bias, (0, pad)).reshape(1, LANES)
    alog = jnp.pad(a_log, (0, pad)).reshape(1, LANES)
    dtbp = jnp.repeat(dt_bias.reshape(npairs, 2), q, axis=1)
    alogp = jnp.repeat(a_log.reshape(npairs, 2), q, axis=1)
    dskip = jnp.repeat(d_skip, SSD_HEAD_DIM).reshape(1, width)
    head_of_col = jnp.arange(width) // SSD_HEAD_DIM
    expand = (jnp.arange(LANES)[:, None] == head_of_col[None, :]).astype(F32)
    ii = jnp.arange(q)
    tri = (ii[:, None] >= ii[None, :]).astype(F32)
    jj = jnp.arange(LANES)
    bd2 = ((jj[:, None] // q == jj[None, :] // q) & (jj[:, None] % q <= jj[None, :] % q)).astype(F32)
    const = lambda bi, j: (0, 0)
    return pl.pallas_call(
        functools.partial(_ssd_kernel, ts=ts, taps=taps, width=width, nstate=nstate),
        grid=(batch, nt),
        in_specs=[pl.BlockSpec((ts, width), lambda bi, j: (bi * nt + j, z_blk)),
                  pl.BlockSpec((ts, cx), lambda bi, j: (bi * nt + j, x_blk)),
                  pl.BlockSpec((ts, LANES), lambda bi, j: (bi * nt + j, 0)),
                  pl.BlockSpec((npairs, 2 * ts), lambda bi, j: (0, bi * nt + j)),
                  pl.BlockSpec((taps, cx), const),
                  pl.BlockSpec((1, cx), const),
                  pl.BlockSpec((1, LANES), const),
                  pl.BlockSpec((1, LANES), const),
                  pl.BlockSpec((npairs, LANES), const),
                  pl.BlockSpec((npairs, LANES), const),
                  pl.BlockSpec((1, width), const),
                  pl.BlockSpec((1, width), const),
                  pl.BlockSpec((LANES, width), const),
                  pl.BlockSpec((q, q), const),
                  pl.BlockSpec((LANES, LANES), const)],
        out_specs=pl.BlockSpec((ts, width), lambda bi, j: (bi * nt + j, 0)),
        out_shape=jax.ShapeDtypeStruct((t, width), F32),
        scratch_shapes=[pltpu.VMEM((SSD_HALO + ts, cx), F32),
                        pltpu.VMEM((SSD_GROUPS, nstate, width // SSD_GROUPS), F32),
                        pltpu.VMEM((ts, width), F32)],
        compiler_params=_params(("arbitrary", "arbitrary"), 48),
        name="ssd_mixer",
    )(proj, proj, dt_raw, dtp, cw, cb.reshape(1, cx), dtb, alog, dtbp, alogp, dskip,
      norm_g.reshape(1, width), expand, tri, bd2)


def _outproj_kernel(yc_ref, ys_ref, h_ref, w1_ref, w2_ref, g_ref, b_ref, o_ref, *, alpha):
    mix = _dot(yc_ref[...].astype(BF16), w1_ref[...]) + _dot(ys_ref[...].astype(BF16), w2_ref[...])
    o_ref[...] = _layer_norm(alpha * h_ref[...] + mix, g_ref[...], b_ref[...])


def _outproj_call(y_conv, y_ssd, h, w1, w2, g, b, alpha, tm):
    t, d = h.shape
    c1 = y_conv.shape[1]
    c2 = y_ssd.shape[1]
    const = lambda i: (0, 0)
    return pl.pallas_call(
        functools.partial(_outproj_kernel, alpha=alpha),
        grid=(t // tm,),
        in_specs=[pl.BlockSpec((tm, c1), lambda i: (i, 0)),
                  pl.BlockSpec((tm, c2), lambda i: (i, 0)),
                  pl.BlockSpec((tm, d), lambda i: (i, 0)),
                  pl.BlockSpec((c1, d), const),
                  pl.BlockSpec((c2, d), const),
                  pl.BlockSpec((1, d), const),
                  pl.BlockSpec((1, d), const)],
        out_specs=pl.BlockSpec((tm, d), lambda i: (i, 0)),
        out_shape=jax.ShapeDtypeStruct((t, d), F32),
        compiler_params=_params(("arbitrary",), 48),
        name="outproj_ln",
    )(y_conv, y_ssd, h, w1, w2, g.reshape(1, d), b.reshape(1, d))


def _topk_rows(s, iota, nvals, k, store):
    for r in range(k):
        m = jnp.max(s, axis=0, keepdims=True)
        idx = jnp.min(jnp.where(s == m, iota, nvals), axis=0, keepdims=True)
        store(r, m, idx)
        s = jnp.where(iota == idx, NEG_INF, s)


def _route_kernel(h_ref, wq_ref, keys_ref, idx_ref, gate_ref, q_scr, ts_scr, ti_scr, bs_scr, bj_scr,
                  *, tm, heads, nkeys, half):
    k = PEER_TOPK
    q_scr[...] = _dot(h_ref[...].astype(BF16), wq_ref[...])
    iota1 = lax.broadcasted_iota(jnp.int32, (nkeys, tm), 0)
    row8 = lax.broadcasted_iota(jnp.int32, (SUBLANES, LANES), 0)
    flat_idx = [lax.broadcasted_iota(jnp.int32, (k, LANES), 0)]
    flat_idx += [k1 * k + row8 for k1 in range(1, SUBLANES)]
    flat_idx.append((SUBLANES + row8) * k)
    flat_idx = jnp.concatenate(flat_idx, axis=0)
    for hd in range(heads):
        for side in range(2):
            hs = hd * 2 + side
            qh = q_scr[:, hs * half:(hs + 1) * half].astype(BF16)
            s = _dot_nt(keys_ref[hs], qh)

            def store1(r, m, idx, side=side):
                ts_scr[side, r:r + 1, :] = m
                ti_scr[side, r:r + 1, :] = idx

            _topk_rows(s, iota1, nkeys, k, store1)
        for lg in range(tm // LANES):
            ls = slice(lg * LANES, (lg + 1) * LANES)
            s0 = ts_scr[0, :, ls]
            s1 = ts_scr[1, :, ls]
            cand = [s0[0:1, :] + s1]
            for k1 in range(1, SUBLANES):
                v = s0[k1:k1 + 1, :] + s1[0:SUBLANES, :]
                cand.append(jnp.where(row8 < k // (k1 + 1), v, NEG_INF))
            cand.append(s0[SUBLANES:, :] + s1[0:1, :])
            cand = jnp.concatenate(cand, axis=0)

            def store2(r, m, idx):
                bs_scr[r:r + 1, :] = m
                bj_scr[r:r + 1, :] = idx

            _topk_rows(cand, flat_idx, k * k, k, store2)
            best_s = bs_scr[...]
            best_j = bj_scr[...]
            k1 = best_j // k
            k2 = best_j % k
            i0 = ti_scr[0, :, ls]
            i1 = ti_scr[1, :, ls]
            e1 = jnp.zeros((k, LANES), jnp.int32)
            e2 = jnp.zeros((k, LANES), jnp.int32)
            for kk in range(k):
                e1 = e1 + jnp.where(k1 == kk, i0[kk:kk + 1, :], 0)
                e2 = e2 + jnp.where(k2 == kk, i1[kk:kk + 1, :], 0)
            ex = jnp.exp(best_s - best_s[0:1, :])
            gate = ex / jnp.sum(ex, axis=0, keepdims=True)
            idx_ref[hd * k:(hd + 1) * k, ls] = e1 * nkeys + e2
            gate_ref[hd * k:(hd + 1) * k, ls] = gate


def _route_call(h, wq, keys, tm):
    t, d = h.shape
    heads, _, nkeys, half = keys.shape
    qd = wq.shape[1]
    k = PEER_TOPK
    assert k == 2 * SUBLANES
    keys2 =keys.reshape(heads * 2, nkeys, half).astype(BF16)
    return pl.pallas_call(
        functools.partial(_route_kernel, tm=tm, heads=heads, nkeys=nkeys, half=half),
        grid=(t // tm,),
        in_specs=[pl.BlockSpec((tm, d), lambda i: (i, 0)),
                  pl.BlockSpec((d, qd), lambda i: (0, 0)),
                  pl.BlockSpec((heads * 2, nkeys, half), lambda i: (0, 0, 0))],
        out_specs=[pl.BlockSpec((heads * k, tm), lambda i: (0, i)),
                   pl.BlockSpec((heads * k, tm), lambda i: (0, i))],
        out_shape=[jax.ShapeDtypeStruct((heads * k, t), jnp.int32),
                   jax.ShapeDtypeStruct((heads * k, t), F32)],
        scratch_shapes=[pltpu.VMEM((tm, qd), F32),
                        pltpu.VMEM((2, k, tm), F32),
                        pltpu.VMEM((2, k, tm), jnp.int32),
                        pltpu.VMEM((k, LANES), F32),
                        pltpu.VMEM((k, LANES), jnp.int32)],
        compiler_params=_params(("arbitrary",), 48),
        name="peer_route",
    )(h, wq, keys2)


def _gelu(v):
    return 0.5 * v * (1.0 + lax.erf(v * (2.0 ** -0.5)))


def _pack_expert_tables(u_tab, v_tab):
    e, d = u_tab.shape
    half = d // 2

    def pack(tab):
        bits = lax.bitcast_convert_type(tab.astype(BF16), jnp.uint16).astype(jnp.uint32)
        words = bits[:, :half] | (bits[:, half:] << 16)
        return lax.bitcast_convert_type(words, jnp.int32).reshape(e, half // LANES, LANES)

    return jnp.concatenate([pack(u_tab), pack(v_tab)], axis=1)


def _unpack_words(w):
    lo = lax.bitcast_convert_type(w << 16, F32)
    hi = lax.bitcast_convert_type(w & jnp.int32(-65536), F32)
    return lo, hi


def _gather_kernel(idx_cur_ref, idx_nxt_ref, x_ref, g_ref, tab_hbm, o_ref, buf0, buf1, w_scr, sem,
                   *, tb, ne, d):
    j = pl.program_id(0)
    nsteps = pl.num_programs(0)
    groups = ne // SUBLANES
    nc = d // (2 * LANES)
    bufs = (buf0, buf1)

    def row_copy(idx_ref, row, t, k, dst):
        e = idx_ref[row, k]
        return pltpu.make_async_copy(tab_hbm.at[e],
                                     bufs[dst].at[t * groups + k // SUBLANES, :, k % SUBLANES, :],
                                     sem.at[dst])

    def block_wait(src):
        pltpu.make_async_copy(bufs[src], bufs[src], sem.at[src]).wait()

    @pl.when(j == 0)
    def _():
        def body(t, carry):
            for k in range(ne):
                row_copy(idx_cur_ref, t, t, k, 0).start()
            return carry
        lax.fori_loop(0, tb, body, 0)

    def block(par, nxt_ref, nxt_row0):
        buf = bufs[par]
        block_wait(par)
        fetches = [(t, k) for t in range(tb) for k in range(ne)]
        per_phase = len(fetches) // (2 * tb)

        def fetch_some(phase):
            for t, k in fetches[phase * per_phase:(phase + 1) * per_phase]:
                row_copy(nxt_ref, nxt_row0 + t, t, k, 1 - par).start()

        cols = []
        for t in range(tb):
            x = x_ref[par * tb + t:par * tb + t + 1, :]
            xs = [x[:, c * LANES:(c + 1) * LANES] for c in range(2 * nc)]
            parts = []
            for g in range(groups):
                acc = None
                for c in range(nc):
                    lo, hi = _unpack_words(buf[t * groups + g, c])
                    term = lo * xs[c] + hi * xs[nc + c]
                    acc = term if acc is None else acc + term
                parts.append(acc)
            cols.append(jnp.sum(jnp.concatenate(parts, axis=0), axis=-1, keepdims=True))
            fetch_some(t)
        act = jnp.concatenate(cols, axis=1)
        w_scr[par] = g_ref[par] * _gelu(act)
        for t in range(tb):
            acc_lo = [None] * nc
            acc_hi = [None] * nc
            for g in range(groups):
                wb = jnp.broadcast_to(w_scr[par, g * SUBLANES:(g + 1) * SUBLANES, t:t + 1],
                                      (SUBLANES, LANES))
                for c in range(nc):
                    lo, hi = _unpack_words(buf[t * groups + g, nc + c])
                    acc_lo[c] = lo * wb if acc_lo[c] is None else acc_lo[c] + lo * wb
                    acc_hi[c] = hi * wb if acc_hi[c] is None else acc_hi[c] + hi * wb
            o_ref[par * tb + t:par * tb + t + 1, :] = jnp.concatenate(
                [jnp.sum(a, axis=0, keepdims=True) for a in acc_lo + acc_hi], axis=1)
            fetch_some(tb + t)

    block(0, idx_cur_ref, tb)
    block(1, idx_nxt_ref, 0)

    @pl.when(j == nsteps - 1)
    def _():
        block_wait(0)


def _gather_call(h, idx, gate3, table, tb):
    t, d = h.shape
    ne = idx.shape[1]
    ns = t // (2 * tb)
    nrows = table.shape[1]
    return pl.pallas_call(
        functools.partial(_gather_kernel, tb=tb, ne=ne, d=d),
        grid=(ns,),
        in_specs=[pl.BlockSpec((2 * tb, ne), lambda j: (j, 0), memory_space=pltpu.SMEM),
                  pl.BlockSpec((2 * tb, ne), lambda j: (jnp.minimum(j + 1, ns - 1), 0),
                               memory_space=pltpu.SMEM),
                  pl.BlockSpec((2 * tb, d), lambda j: (j, 0)),
                  pl.BlockSpec((2, ne, tb), lambda j: (j, 0, 0)),
                  pl.BlockSpec(memory_space=pl.ANY)],
        out_specs=pl.BlockSpec((2 * tb, d), lambda j: (j, 0)),
        out_shape=jax.ShapeDtypeStruct((t, d), F32),
        scratch_shapes=[pltpu.VMEM((tb * ne // SUBLANES, nrows, SUBLANES, LANES), jnp.int32),
                        pltpu.VMEM((tb * ne // SUBLANES, nrows, SUBLANES, LANES), jnp.int32),
                        pltpu.VMEM((2, ne, tb), F32),
                        pltpu.SemaphoreType.DMA((2,))],
        compiler_params=_params(("arbitrary",), 48),
        name="peer_gather",
    )(idx, idx, h, gate3, table)


def _ple_kernel(h_ref, f_ref, p_ref, wg_ref, wp_ref, g_ref, b_ref, o_ref, *, alpha):
    h = h_ref[...]
    gate = _sigmoid(_dot(h.astype(BF16), wg_ref[...]))
    emb = _dot(p_ref[...].astype(BF16), wp_ref[...])
    o_ref[...] = _layer_norm(alpha * h + f_ref[...] + gate * emb, g_ref[...], b_ref[...])


def _ple_call(h, ffn, p, wg, wp, g, b, alpha, tm):
    t, d = h.shape
    pd = p.shape[1]
    const = lambda i: (0, 0)
    return pl.pallas_call(
        functools.partial(_ple_kernel, alpha=alpha),
        grid=(t // tm,),
        in_specs=[pl.BlockSpec((tm, d), lambda i: (i, 0)),
                  pl.BlockSpec((tm, d), lambda i: (i, 0)),
                  pl.BlockSpec((tm, pd), lambda i: (i, 0)),
                  pl.BlockSpec((d, d), const),
                  pl.BlockSpec((pd, d), const),
                  pl.BlockSpec((1, d), const),
                  pl.BlockSpec((1, d), const)],
        out_specs=pl.BlockSpec((tm, d), lambda i: (i, 0)),
        out_shape=jax.ShapeDtypeStruct((t, d), F32),
        compiler_params=_params(("arbitrary",), 48),
        name="ple_ln",
    )(h, ffn, p, wg, wp, g.reshape(1, d), b.reshape(1, d))


def _tile(n, want):
    return want if n % want == 0 else n


def kernel(x, p, ln_in_g, ln_in_b, w_in, conv_w, conv_b, conv_ln_g, conv_ln_b, ssd_conv_w, ssd_conv_b, dt_bias, a_log, d_skip, ssd_norm_g, w_out, ln1_g, ln1_b, peer_wq, peer_keys, peer_u, peer_v, ple_w_gate, ple_w_proj, ln2_g, ln2_b):
    batch, seq, d = x.shape
    depth = w_in.shape[0]
    t = batch * seq
    conv_width = conv_w.shape[2]
    width = ssd_norm_g.shape[1]
    cx = ssd_conv_w.shape[2]
    heads = dt_bias.shape[1]
    n_main = 2 * conv_width + width + cx
    alpha = float((2 * depth) ** 0.25)
    q = SSD_CHUNK

    tm = _tile(t, 256)
    ts = _tile(seq, 256)
    tb = 8
    tn_main = n_main // 3 if (n_main // 3) % LANES == 0 and n_main % 3 == 0 else n_main

    h = _ln_call(x.reshape(t, d), ln_in_g, ln_in_b, tm)
    p2 = p.reshape(depth, t, p.shape[-1])
    for i in range(depth):
        w_main = w_in[i, :, :n_main].astype(BF16)
        w_dt = jnp.pad(w_in[i, :, n_main:], ((0, 0), (0, LANES - heads))).astype(BF16)
        proj = _matmul_call(h, w_main, _tile(t, 512), tn_main, "in_proj")
        dt_raw = _matmul_call(h, w_dt, _tile(t, 512), LANES, "dt_proj")
        dtp = (dt_raw[:, :heads].reshape(t // q, q, heads // 2, 2)
               .transpose(2, 0, 3, 1).reshape(heads // 2, 2 * t))
        y_conv = _conv_call(proj, conv_w[i], conv_b[i], conv_ln_g[i], conv_ln_b[i], batch, seq, ts)
        y_ssd = _ssd_call(proj, dt_raw, dtp, ssd_conv_w[i], ssd_conv_b[i], dt_bias[i], a_log[i],
                          d_skip[i], ssd_norm_g[i], batch, seq, ts, conv_width)
        w_o = w_out[i].astype(BF16)
        h = _outproj_call(y_conv, y_ssd, h, w_o[:conv_width], w_o[conv_width:], ln1_g[i], ln1_b[i],
                          alpha, tm)
        idx_t, gate_t = _route_call(h, peer_wq[i].astype(BF16), peer_keys[i], tm)
        ne = idx_t.shape[0]
        idx = idx_t.T
        gate3 = gate_t.reshape(ne, t // tb, tb).transpose(1, 0, 2)
        ffn = _gather_call(h, idx, gate3, _pack_expert_tables(peer_u[i], peer_v[i]), tb)
        h = _ple_call(h, ffn, p2[i], ple_w_gate[i].astype(BF16), ple_w_proj[i].astype(BF16),
                      ln2_g[i], ln2_b[i], alpha, tm)
    return h.reshape(batch, seq, d)
```

```python
import functools

import jax
import jax.numpy as jnp
from jax import lax
from jax.experimental import pallas as pl
from jax.experimental.pallas import tpu as pltpu

LN_EPS = 1e-5
RMS_EPS = 1e-5
SSD_GROUPS = 2
SSD_HEAD_DIM = 64
SSD_CHUNK = 64
PEER_TOPK = 16
LANES = 128
SUBLANES = 8
F32 = jnp.float32
BF16 = jnp.bfloat16
HIGHEST = lax.Precision.HIGHEST
NEG_INF = float("-inf")
MIB = 1024 * 1024


def _params(sem, vmem_mib):
    return pltpu.CompilerParams(dimension_semantics=sem, vmem_limit_bytes=vmem_mib * MIB)


def _layer_norm(v, g, b):
    mu = jnp.mean(v, axis=-1, keepdims=True)
    d = v - mu
    var = jnp.mean(d * d, axis=-1, keepdims=True)
    return d * lax.rsqrt(var + LN_EPS) * g + b


def _sigmoid(v):
    return 1.0 / (1.0 + jnp.exp(-v))


def _dot(a, b):
    return jnp.dot(a, b, preferred_element_type=F32)


def _dot_nt(a, b):
    return lax.dot_general(a, b, (((1,), (1,)), ((), ())), preferred_element_type=F32)


def _dot_exact(a, b):
    return jnp.dot(a, b, preferred_element_type=F32, precision=HIGHEST)


def _ln_kernel(x_ref, g_ref, b_ref, o_ref):
    o_ref[...] = _layer_norm(x_ref[...], g_ref[...], b_ref[...])


def _ln_call(x, g, b, tm):
    t, d = x.shape
    return pl.pallas_call(
        _ln_kernel,
        grid=(t // tm,),
        in_specs=[pl.BlockSpec((tm, d), lambda i: (i, 0)),
                  pl.BlockSpec((1, d), lambda i: (0, 0)),
                  pl.BlockSpec((1, d), lambda i: (0, 0))],
        out_specs=pl.BlockSpec((tm, d), lambda i: (i, 0)),
        out_shape=jax.ShapeDtypeStruct((t, d), F32),
        compiler_params=_params(("arbitrary",), 32),
        name="ln_in",
    )(x, g.reshape(1, d), b.reshape(1, d))


def _matmul_kernel(x_ref, w_ref, o_ref):
    o_ref[...] = _dot(x_ref[...].astype(BF16), w_ref[...])


def _matmul_call(x, w, tm, tn, name):
    m, k = x.shape
    n = w.shape[1]
    return pl.pallas_call(
        _matmul_kernel,
        grid=(n // tn, m // tm),
        in_specs=[pl.BlockSpec((tm, k), lambda j, i: (i, 0)),
                  pl.BlockSpec((k, tn), lambda j, i: (0, j))],
        out_specs=pl.BlockSpec((tm, tn), lambda j, i: (i, j)),
        out_shape=jax.ShapeDtypeStruct((m, n), F32),
        compiler_params=_params(("arbitrary", "arbitrary"), 48),
        name=name,
    )(x, w)


CONV_HALO = 32
CONV_ROWS = 16


def _conv_kernel(a_ref, g_ref, w_ref, b_ref, lg_ref, lb_ref, o_ref, ubuf, shifted, *, ts, taps):
    rows = CONV_HALO + ts

    @pl.when(pl.program_id(1) == 0)
    def _():
        ubuf[0:CONV_HALO, :] = jnp.zeros((CONV_HALO, ubuf.shape[1]), F32)
        ubuf[rows:rows + SUBLANES, :] = jnp.zeros((SUBLANES, ubuf.shape[1]), F32)

    ubuf[CONV_HALO:rows, :] = a_ref[...] * _sigmoid(g_ref[...])
    for s in range(1, SUBLANES):
        shifted[s - 1] = ubuf[s:s + rows, :]
    bias = b_ref[...]
    lg = lg_ref[...]
    lb = lb_ref[...]
    for r0 in range(0, ts, CONV_ROWS):
        acc = jnp.zeros((CONV_ROWS, ubuf.shape[1]), F32) + bias
        for k in range(taps):
            off = CONV_HALO - (taps - 1) + k + r0
            s = off % SUBLANES
            base = off - s
            if s == 0:
                win = ubuf[base:base + CONV_ROWS, :]
            else:
                win = shifted[s - 1, base:base + CONV_ROWS, :]
            acc = acc + win * w_ref[k:k + 1, :]
        y = _layer_norm(acc, lg, lb)
        o_ref[r0:r0 + CONV_ROWS, :] = y * _sigmoid(y)
    ubuf[0:CONV_HALO, :] = ubuf[ts:ts + CONV_HALO, :]


def _conv_call(proj, w, b, lg, lb, batch, seq, ts):
    taps, c = w.shape
    nt = seq // ts
    t = batch * seq
    return pl.pallas_call(
        functools.partial(_conv_kernel, ts=ts, taps=taps),
        grid=(batch, nt),
        in_specs=[pl.BlockSpec((ts, c), lambda bi, j: (bi * nt + j, 0)),
                  pl.BlockSpec((ts, c), lambda bi, j: (bi * nt + j, 1)),
                  pl.BlockSpec((taps, c), lambda bi, j: (0, 0)),
                  pl.BlockSpec((1, c), lambda bi, j: (0, 0)),
                  pl.BlockSpec((1, c), lambda bi, j: (0, 0)),
                  pl.BlockSpec((1, c), lambda bi, j: (0, 0))],
        out_specs=pl.BlockSpec((ts, c), lambda bi, j: (bi * nt + j, 0)),
        out_shape=jax.ShapeDtypeStruct((t, c), F32),
        scratch_shapes=[pltpu.VMEM((CONV_HALO + ts + SUBLANES, c), F32),
                        pltpu.VMEM((SUBLANES - 1, CONV_HALO + ts, c), F32)],
        compiler_params=_params(("arbitrary", "arbitrary"), 48),
        name="conv_module",
    )(proj, proj, w, b.reshape(1, c), lg.reshape(1, c), lb.reshape(1, c))


SSD_HALO = 8


def _softplus(v):
    return jnp.maximum(v, 0.0) + jnp.log(1.0 + jnp.exp(-jnp.abs(v)))


def _ssd_kernel(z_ref, xbc_ref, dt_ref, dtp_ref, cw_ref, cb_ref, dtb_ref, alog_ref, dtbp_ref,
                alogp_ref, dskip_ref, ng_ref, exp_ref, tri_ref, bd2_ref, o_ref,
                xbuf, prev, ybuf, *, ts, taps, width, nstate):
    q = SSD_CHUNK
    gw = width // SSD_GROUPS
    pairs_per_group = gw // LANES

    @pl.when(pl.program_id(1) == 0)
    def _():
        xbuf[0:SSD_HALO, :] = jnp.zeros((SSD_HALO, xbuf.shape[1]), F32)
        prev[...] = jnp.zeros(prev.shape, F32)

    xbuf[SSD_HALO:SSD_HALO + ts, :] = xbc_ref[...]

    a_neg = -jnp.exp(alog_ref[...])
    dt = _softplus(dt_ref[...] + dtb_ref[...])
    expand = exp_ref[...]
    dtx = _dot_exact(dt, expand)
    dax = _dot_exact(dt * a_neg, expand)
    a_neg_p = -jnp.exp(alogp_ref[...])
    tri = tri_ref[...]
    bd2 = bd2_ref[...]
    lane = lax.broadcasted_iota(jnp.int32, (q, LANES), 1)
    row = lax.broadcasted_iota(jnp.int32, (q, LANES), 0)
    causal2 = row >= (lane % q)
    left = lane < q

    for c in range(ts // q):
        r0 = c * q
        xc = jnp.zeros((q, xbuf.shape[1]), F32) + cb_ref[...]
        for k in range(taps):
            off = SSD_HALO - (taps - 1) + k + r0
            xc = xc + xbuf[off:off + q, :] * cw_ref[k:k + 1, :]
        xc = xc * _sigmoid(xc)
        xs = xc[:, :width]
        bm = xc[:, width:width + SSD_GROUPS * nstate]
        cm = xc[:, width + SSD_GROUPS * nstate:]

        acx = _dot_exact(tri, dax[r0:r0 + q, :])
        dap = _softplus(dtp_ref[:, c * LANES:(c + 1) * LANES] + dtbp_ref[...]) * a_neg_p
        arow = _dot_exact(dap, bd2)
        last = acx[q - 1:q, :]
        x_dt = xs * dtx[r0:r0 + q, :]
        x_state = (x_dt * jnp.exp(last - acx)).astype(BF16)
        x_dt_b = x_dt.astype(BF16)
        exp_ac = jnp.exp(acx)
        chunk_decay = jnp.exp(last)
        skip = xs * dskip_ref[...]

        for g in range(SSD_GROUPS):
            gs = slice(g * gw, (g + 1) * gw)
            bg = bm[:, g * nstate:(g + 1) * nstate].astype(BF16)
            cg = cm[:, g * nstate:(g + 1) * nstate].astype(BF16)
            cb2 = _dot_nt(cg, jnp.concatenate([bg, bg], axis=0))
            prev_g = prev[g]
            y_off = _dot(cg, prev_g.astype(BF16)) * exp_ac[:, gs]
            states = lax.dot_general(bg, x_state[:, gs], (((0,), (0,)), ((), ())),
                                     preferred_element_type=F32)
            prev[g] = prev_g * chunk_decay[:, gs] + states
            for pr in range(pairs_per_group):
                pi = g * pairs_per_group + pr
                cols = slice(pi * LANES, (pi + 1) * LANES)
                seg = acx[:, cols] - arow[pi:pi + 1, :]
                decay = jnp.exp(jnp.where(causal2, seg, NEG_INF))
                m2 = (cb2 * decay).astype(BF16)
                xp = x_dt_b[:, cols]
                zero = jnp.zeros_like(xp)
                bdx = jnp.concatenate([jnp.where(left, xp, zero), jnp.where(left, zero, xp)], axis=0)
                y_diag = _dot(m2, bdx)
                ybuf[r0:r0 + q, cols] = (y_diag + y_off[:, pr * LANES:(pr + 1) * LANES] + skip[:, cols])

    xbuf[0:SSD_HALO, :] = xbuf[ts:ts + SSD_HALO, :]

    z = z_ref[...]
    yg = ybuf[...] * (z * _sigmoid(z))
    ng = ng_ref[...]
    for g in range(SSD_GROUPS):
        gs = slice(g * gw, (g + 1) * gw)
        v = yg[:, gs]
        ms = jnp.mean(v * v, axis=-1, keepdims=True)
        o_ref[:, gs] = v * lax.rsqrt(ms + RMS_EPS) * ng[:, gs]


def _ssd_call(proj, dt_raw, dtp, cw, cb, dt_bias, a_log, d_skip, norm_g, batch, seq, ts, conv_width):
    taps, cx = cw.shape
    width = norm_g.shape[0]
    heads = dt_bias.shape[0]
    nstate = (cx - width) // (2 * SSD_GROUPS)
    npairs = heads // 2
    q = SSD_CHUNK
    nt = seq // ts
    t = batch * seq
    z_blk = (2 * conv_width) // width
    x_blk = (2 * conv_width + width) // cx
    assert z_blk * width == 2 * conv_width and x_blk * cx == 2 * conv_width + width
    pad = LANES - heads
    dtb = jnp.pad(dt_bias, (0, pad)).reshape(1, LANES)
    alog = jnp.pad(a_log, (0, pad)).reshape(1, LANES)
    dtbp = jnp.repeat(dt_bias.reshape(npairs, 2), q, axis=1)
    alogp = jnp.repeat(a_log.reshape(npairs, 2), q, axis=1)
    dskip = jnp.repeat(d_skip, SSD_HEAD_DIM).reshape(1, width)
    head_of_col = jnp.arange(width) // SSD_HEAD_DIM
    expand = (jnp.arange(LANES)[:, None] == head_of_col[None, :]).astype(F32)
    ii = jnp.arange(q)
    tri = (ii[:, None] >= ii[None, :]).astype(F32)
    jj = jnp.arange(LANES)
    bd2 = ((jj[:, None] // q == jj[None, :] // q) & (jj[:, None] % q <= jj[None, :] % q)).astype(F32)
    const = lambda bi, j: (0, 0)
    return pl.pallas_call(
        functools.partial(_ssd_kernel, ts=ts, taps=taps, width=width, nstate=nstate),
        grid=(batch, nt),
        in_specs=[pl.BlockSpec((ts, width), lambda bi, j: (bi * nt + j, z_blk)),
                  pl.BlockSpec((ts, cx), lambda bi, j: (bi * nt + j, x_blk)),
                  pl.BlockSpec((ts, LANES), lambda bi, j: (bi * nt + j, 0)),
                  pl.BlockSpec((npairs, 2 * ts), lambda bi, j: (0, bi * nt + j)),
                  pl.BlockSpec((taps, cx), const),
                  pl.BlockSpec((1, cx), const),
                  pl.BlockSpec((1, LANES), const),
                  pl.BlockSpec((1, LANES), const),
                  pl.BlockSpec((npairs, LANES), const),
                  pl.BlockSpec((npairs, LANES), const),
                  pl.BlockSpec((1, width), const),
                  pl.BlockSpec((1, width), const),
                  pl.BlockSpec((LANES, width), const),
                  pl.BlockSpec((q, q), const),
                  pl.BlockSpec((LANES, LANES), const)],
        out_specs=pl.BlockSpec((ts, width), lambda bi, j: (bi * nt + j, 0)),
        out_shape=jax.ShapeDtypeStruct((t, width), F32),
        scratch_shapes=[pltpu.VMEM((SSD_HALO + ts, cx), F32),
                        pltpu.VMEM((SSD_GROUPS, nstate, width // SSD_GROUPS), F32),
                        pltpu.VMEM((ts, width), F32)],
        compiler_params=_params(("arbitrary", "arbitrary"), 48),
        name="ssd_mixer",
    )(proj, proj, dt_raw, dtp, cw, cb.reshape(1, cx), dtb, alog, dtbp, alogp, dskip,
      norm_g.reshape(1, width), expand, tri, bd2)


def _outproj_kernel(yc_ref, ys_ref, h_ref, w1_ref, w2_ref, g_ref, b_ref, o_ref, *, alpha):
    mix = _dot(yc_ref[...].astype(BF16), w1_ref[...]) + _dot(ys_ref[...].astype(BF16), w2_ref[...])
    o_ref[...] = _layer_norm(alpha * h_ref[...] + mix, g_ref[...], b_ref[...])


def _outproj_call(y_conv, y_ssd, h, w1, w2, g, b, alpha, tm):
    t, d = h.shape
    c1 = y_conv.shape[1]
    c2 = y_ssd.shape[1]
    const = lambda i: (0, 0)
    return pl.pallas_call(
        functools.partial(_outproj_kernel, alpha=alpha),
        grid=(t // tm,),
        in_specs=[pl.BlockSpec((tm, c1), lambda i: (i, 0)),
                  pl.BlockSpec((tm, c2), lambda i: (i, 0)),
                  pl.BlockSpec((tm, d), lambda i: (i, 0)),
                  pl.BlockSpec((c1, d), const),
                  pl.BlockSpec((c2, d), const),
                  pl.BlockSpec((1, d), const),
                  pl.BlockSpec((1, d), const)],
        out_specs=pl.BlockSpec((tm, d), lambda i: (i, 0)),
        out_shape=jax.ShapeDtypeStruct((t, d), F32),
        compiler_params=_params(("arbitrary",), 48),
        name="outproj_ln",
    )(y_conv, y_ssd, h, w1, w2, g.reshape(1, d), b.reshape(1, d))


def _topk_rows(s, iota, nvals, k, store):
    for r in range(k):
        m = jnp.max(s, axis=0, keepdims=True)
        idx = jnp.min(jnp.where(s == m, iota, nvals), axis=0, keepdims=True)
        store(r, m, idx)
        s = jnp.where(iota == idx, NEG_INF, s)


def _route_kernel(h_ref, wq_ref, keys_ref, idx_ref, gate_ref, q_scr, ts_scr, ti_scr, bs_scr, bj_scr,
                  *, tm, heads, nkeys, half):
    k = PEER_TOPK
    q_scr[...] = _dot(h_ref[...].astype(BF16), wq_ref[...])
    iota1 = lax.broadcasted_iota(jnp.int32, (nkeys, tm), 0)
    row8 = lax.broadcasted_iota(jnp.int32, (SUBLANES, LANES), 0)
    flat_idx = [lax.broadcasted_iota(jnp.int32, (k, LANES), 0)]
    flat_idx += [k1 * k + row8 for k1 in range(1, SUBLANES)]
    flat_idx.append((SUBLANES + row8) * k)
    flat_idx = jnp.concatenate(flat_idx, axis=0)
    for hd in range(heads):
        for side in range(2):
            hs = hd * 2 + side
            qh = q_scr[:, hs * half:(hs + 1) * half].astype(BF16)
            s = _dot_nt(keys_ref[hs], qh)

            def store1(r, m, idx, side=side):
                ts_scr[side, r:r + 1, :] = m
                ti_scr[side, r:r + 1, :] = idx

            _topk_rows(s, iota1, nkeys, k, store1)
        for lg in range(tm // LANES):
            ls = slice(lg * LANES, (lg + 1) * LANES)
            s0 = ts_scr[0, :, ls]
            s1 = ts_scr[1, :, ls]
            cand = [s0[0:1, :] + s1]
            for k1 in range(1, SUBLANES):
                v = s0[k1:k1 + 1, :] + s1[0:SUBLANES, :]
                cand.append(jnp.where(row8 < k // (k1 + 1), v, NEG_INF))
            cand.append(s0[SUBLANES:, :] + s1[0:1, :])
            cand = jnp.concatenate(cand, axis=0)

            def store2(r, m, idx):
                bs_scr[r:r + 1, :] = m
                bj_scr[r:r + 1, :] = idx

            _topk_rows(cand, flat_idx, k * k, k, store2)
            best_s = bs_scr[...]
            best_j = bj_scr[...]
            k1 = best_j // k
            k2 = best_j % k
            i0 = ti_scr[0, :, ls]
            i1 = ti_scr[1, :, ls]
            e1 = jnp.zeros((k, LANES), jnp.int32)
            e2 = jnp.zeros((k, LANES), jnp.int32)
            for kk in range(k):
                e1 = e1 + jnp.where(k1 == kk, i0[kk:kk + 1, :], 0)
                e2 = e2 + jnp.where(k2 == kk, i1[kk:kk + 1, :], 0)
            ex = jnp.exp(best_s - best_s[0:1, :])
            gate = ex / jnp.sum(ex, axis=0, keepdims=True)
            idx_ref[hd * k:(hd + 1) * k, ls] = e1 * nkeys + e2
            gate_ref[hd * k:(hd + 1) * k, ls] = gate


def _route_call(h, wq, keys, tm):
    t, d = h.shape
    heads, _, nkeys, half = keys.shape
    qd = wq.shape[1]
    k = PEER_TOPK
    assert k == 2 * SUBLANES
    keys2 =keys.reshape(heads * 2, nkeys, half).astype(BF16)
    return pl.pallas_call(
        functools.partial(_route_kernel, tm=tm, heads=heads, nkeys=nkeys, half=half),
        grid=(t // tm,),
        in_specs=[pl.BlockSpec((tm, d), lambda i: (i, 0)),
                  pl.BlockSpec((d, qd), lambda i: (0, 0)),
                  pl.BlockSpec((heads * 2, nkeys, half), lambda i: (0, 0, 0))],
        out_specs=[pl.BlockSpec((heads * k, tm), lambda i: (0, i)),
                   pl.BlockSpec((heads * k, tm), lambda i: (0, i))],
        out_shape=[jax.ShapeDtypeStruct((heads * k, t), jnp.int32),
                   jax.ShapeDtypeStruct((heads * k, t), F32)],
        scratch_shapes=[pltpu.VMEM((tm, qd), F32),
                        pltpu.VMEM((2, k, tm), F32),
                        pltpu.VMEM((2, k, tm), jnp.int32),
                        pltpu.VMEM((k, LANES), F32),
                        pltpu.VMEM((k, LANES), jnp.int32)],
        compiler_params=_params(("arbitrary",), 48),
        name="peer_route",
    )(h, wq, keys2)


def _gelu(v):
    return 0.5 * v * (1.0 + lax.erf(v * (2.0 ** -0.5)))


def _pack_expert_tables(u_tab, v_tab):
    e, d = u_tab.shape
    half = d // 2

    def pack(tab):
        bits = lax.bitcast_convert_type(tab.astype(BF16), jnp.uint16).astype(jnp.uint32)
        words = bits[:, :half] | (bits[:, half:] << 16)
        return lax.bitcast_convert_type(words, jnp.int32).reshape(e, half // LANES, LANES)

    return jnp.concatenate([pack(u_tab), pack(v_tab)], axis=1)


def _unpack_words(w):
    lo = lax.bitcast_convert_type(w << 16, F32)
    hi = lax.bitcast_convert_type(w & jnp.int32(-65536), F32)
    return lo, hi


GATHER_BUFFERS = 4
GATHER_AHEAD = 2


def _gather_kernel(idx_cur_ref, idx_nxt_ref, x_ref, g_ref, tab_hbm, o_ref, *scratch, tb, ne, d):
    bufs = scratch[:GATHER_BUFFERS]
    w_scr, sem = scratch[GATHER_BUFFERS:]
    j = pl.program_id(0)
    nsteps = pl.num_programs(0)
    groups = ne // SUBLANES
    nc = d // (2 * LANES)

    def row_copy(idx_ref, row, t, k, dst):
        e = idx_ref[row, k]
        return pltpu.make_async_copy(tab_hbm.at[e],
                                     bufs[dst].at[t * groups + k // SUBLANES, :, k % SUBLANES, :],
                                     sem.at[dst])

    def block_wait(src):
        pltpu.make_async_copy(bufs[src], bufs[src], sem.at[src]).wait()

    @pl.when(j == 0)
    def _():
        def body(t, carry):
            for b in range(GATHER_AHEAD):
                for k in range(ne):
                    row_copy(idx_cur_ref, b * tb + t, t, k, b).start()
            return carry
        lax.fori_loop(0, tb, body, 0)

    def block(par):
        buf = bufs[par]
        block_wait(par)
        ahead = par + GATHER_AHEAD
        nxt_ref = idx_cur_ref if ahead < GATHER_BUFFERS else idx_nxt_ref
        nxt_row0 = (ahead % GATHER_BUFFERS) * tb
        dst = ahead % GATHER_BUFFERS
        fetches = [(t, k) for t in range(tb) for k in range(ne)]
        per_phase = len(fetches) // (2 * tb)

        def fetch_some(phase):
            for t, k in fetches[phase * per_phase:(phase + 1) * per_phase]:
                row_copy(nxt_ref, nxt_row0 + t, t, k, dst).start(priority=k % 2)

        cols = []
        for t in range(tb):
            x = x_ref[par * tb + t:par * tb + t + 1, :]
            xs = [x[:, c * LANES:(c + 1) * LANES] for c in range(2 * nc)]
            parts = []
            for g in range(groups):
                acc = None
                for c in range(nc):
                    lo, hi = _unpack_words(buf[t * groups + g, c])
                    term = lo * xs[c] + hi * xs[nc + c]
                    acc = term if acc is None else acc + term
                parts.append(acc)
            cols.append(jnp.sum(jnp.concatenate(parts, axis=0), axis=-1, keepdims=True))
            fetch_some(t)
        act = jnp.concatenate(cols, axis=1)
        w_scr[par] = g_ref[par] * _gelu(act)
        for t in range(tb):
            acc_lo = [None] * nc
            acc_hi = [None] * nc
            for g in range(groups):
                wb = jnp.broadcast_to(w_scr[par, g * SUBLANES:(g + 1) * SUBLANES, t:t + 1],
                                      (SUBLANES, LANES))
                for c in range(nc):
                    lo, hi = _unpack_words(buf[t * groups + g, nc + c])
                    acc_lo[c] = lo * wb if acc_lo[c] is None else acc_lo[c] + lo * wb
                    acc_hi[c] = hi * wb if acc_hi[c] is None else acc_hi[c] + hi * wb
            o_ref[par * tb + t:par * tb + t + 1, :] = jnp.concatenate(
                [jnp.sum(a, axis=0, keepdims=True) for a in acc_lo + acc_hi], axis=1)
            fetch_some(tb + t)

    for par in range(GATHER_BUFFERS):
        block(par)

    @pl.when(j == nsteps - 1)
    def _():
        for b in range(GATHER_AHEAD):
            block_wait(b)


def _gather_call(h, idx, gate3, table, tb):
    t, d = h.shape
    ne = idx.shape[1]
    rows = GATHER_BUFFERS * tb
    ns = t // rows
    nrows = table.shape[1]
    return pl.pallas_call(
        functools.partial(_gather_kernel, tb=tb, ne=ne, d=d),
        grid=(ns,),
        in_specs=[pl.BlockSpec((rows, ne), lambda j: (j, 0), memory_space=pltpu.SMEM),
                  pl.BlockSpec((rows, ne), lambda j: (jnp.minimum(j + 1, ns - 1), 0),
                               memory_space=pltpu.SMEM),
                  pl.BlockSpec((rows, d), lambda j: (j, 0)),
                  pl.BlockSpec((GATHER_BUFFERS, ne, tb), lambda j: (j, 0, 0)),
                  pl.BlockSpec(memory_space=pl.ANY)],
        out_specs=pl.BlockSpec((rows, d), lambda j: (j, 0)),
        out_shape=jax.ShapeDtypeStruct((t, d), F32),
        scratch_shapes=[pltpu.VMEM((tb * ne // SUBLANES, nrows, SUBLANES, LANES), jnp.int32)
                        for _ in range(GATHER_BUFFERS)] +
                       [pltpu.VMEM((GATHER_BUFFERS, ne, tb), F32),
                        pltpu.SemaphoreType.DMA((GATHER_BUFFERS,))],
        compiler_params=_params(("arbitrary",), 56),
        name="peer_gather",
    )(idx, idx, h, gate3, table)


def _ple_kernel(h_ref, f_ref, p_ref, wg_ref, wp_ref, g_ref, b_ref, o_ref, *, alpha):
    h = h_ref[...]
    gate = _sigmoid(_dot(h.astype(BF16), wg_ref[...]))
    emb = _dot(p_ref[...].astype(BF16), wp_ref[...])
    o_ref[...] = _layer_norm(alpha * h + f_ref[...] + gate * emb, g_ref[...], b_ref[...])


def _ple_call(h, ffn, p, wg, wp, g, b, alpha, tm):
    t, d = h.shape
    pd = p.shape[1]
    const = lambda i: (0, 0)
    return pl.pallas_call(
        functools.partial(_ple_kernel, alpha=alpha),
        grid=(t // tm,),
        in_specs=[pl.BlockSpec((tm, d), lambda i: (i, 0)),
                  pl.BlockSpec((tm, d), lambda i: (i, 0)),
                  pl.BlockSpec((tm, pd), lambda i: (i, 0)),
                  pl.BlockSpec((d, d), const),
                  pl.BlockSpec((pd, d), const),
                  pl.BlockSpec((1, d), const),
                  pl.BlockSpec((1, d), const)],
        out_specs=pl.BlockSpec((tm, d), lambda i: (i, 0)),
        out_shape=jax.ShapeDtypeStruct((t, d), F32),
        compiler_params=_params(("arbitrary",), 48),
        name="ple_ln",
    )(h, ffn, p, wg, wp, g.reshape(1, d), b.reshape(1, d))


def _tile(n, want):
    return want if n % want == 0 else n


def kernel(x, p, ln_in_g, ln_in_b, w_in, conv_w, conv_b, conv_ln_g, conv_ln_b, ssd_conv_w, ssd_conv_b, dt_bias, a_log, d_skip, ssd_norm_g, w_out, ln1_g, ln1_b, peer_wq, peer_keys, peer_u, peer_v, ple_w_gate, ple_w_proj, ln2_g, ln2_b):
    batch, seq, d = x.shape
    depth = w_in.shape[0]
    t = batch * seq
    conv_width = conv_w.shape[2]
    width = ssd_norm_g.shape[1]
    cx = ssd_conv_w.shape[2]
    heads = dt_bias.shape[1]
    n_main = 2 * conv_width + width + cx
    alpha = float((2 * depth) ** 0.25)
    q = SSD_CHUNK

    tm = _tile(t, 256)
    ts = _tile(seq, 256)
    tb = 8
    tn_main = n_main // 3 if (n_main // 3) % LANES == 0 and n_main % 3 == 0 else n_main

    h = _ln_call(x.reshape(t, d), ln_in_g, ln_in_b, tm)
    p2 = p.reshape(depth, t, p.shape[-1])
    for i in range(depth):
        w_main = w_in[i, :, :n_main].astype(BF16)
        w_dt = jnp.pad(w_in[i, :, n_main:], ((0, 0), (0, LANES - heads))).astype(BF16)
        proj = _matmul_call(h, w_main, _tile(t, 512), tn_main, "in_proj")
        dt_raw = _matmul_call(h, w_dt, _tile(t, 512), LANES, "dt_proj")
        dtp = (dt_raw[:, :heads].reshape(t // q, q, heads // 2, 2)
               .transpose(2, 0, 3, 1).reshape(heads // 2, 2 * t))
        y_conv = _conv_call(proj, conv_w[i], conv_b[i], conv_ln_g[i], conv_ln_b[i], batch, seq, ts)
        y_ssd = _ssd_call(proj, dt_raw, dtp, ssd_conv_w[i], ssd_conv_b[i], dt_bias[i], a_log[i],
                          d_skip[i], ssd_norm_g[i], batch, seq, ts, conv_width)
        w_o = w_out[i].astype(BF16)
        h = _outproj_call(y_conv, y_ssd, h, w_o[:conv_width], w_o[conv_width:], ln1_g[i], ln1_b[i],
                          alpha, tm)
        idx_t, gate_t = _route_call(h, peer_wq[i].astype(BF16), peer_keys[i], tm)
        ne = idx_t.shape[0]
        idx = idx_t.T
        gate3 = gate_t.reshape(ne, t // tb, tb).transpose(1, 0, 2)
        ffn = _gather_call(h, idx, gate3, _pack_expert_tables(peer_u[i], peer_v[i]), tb)
        h = _ple_call(h, ffn, p2[i], ple_w_gate[i].astype(BF16), ple_w_proj[i].astype(BF16),
                      ln2_g[i], ln2_b[i], alpha, tm)
    return h.reshape(batch, seq, d)
```

```python
import functools

import jax
import jax.numpy as jnp
from jax import lax
from jax.experimental import pallas as pl
from jax.experimental.pallas import tpu as pltpu

LN_EPS = 1e-5
RMS_EPS = 1e-5
SSD_GROUPS = 2
SSD_HEAD_DIM = 64
SSD_CHUNK = 64
PEER_TOPK = 16
LANES = 128
SUBLANES = 8
F32 = jnp.float32
BF16 = jnp.bfloat16
HIGHEST = lax.Precision.HIGHEST
NEG_INF = float("-inf")
MIB = 1024 * 1024


def _params(sem, vmem_mib):
    return pltpu.CompilerParams(dimension_semantics=sem, vmem_limit_bytes=vmem_mib * MIB)


def _layer_norm(v, g, b):
    mu = jnp.mean(v, axis=-1, keepdims=True)
    d = v - mu
    var = jnp.mean(d * d, axis=-1, keepdims=True)
    return d * lax.rsqrt(var + LN_EPS) * g + b


def _sigmoid(v):
    return 1.0 / (1.0 + jnp.exp(-v))


def _dot(a, b):
    return jnp.dot(a, b, preferred_element_type=F32)


def _dot_nt(a, b):
    return lax.dot_general(a, b, (((1,), (1,)), ((), ())), preferred_element_type=F32)


def _dot_exact(a, b):
    return jnp.dot(a, b, preferred_element_type=F32, precision=HIGHEST)


def _ln_kernel(x_ref, g_ref, b_ref, o_ref):
    o_ref[...] = _layer_norm(x_ref[...], g_ref[...], b_ref[...])


def _ln_call(x, g, b, tm):
    t, d = x.shape
    return pl.pallas_call(
        _ln_kernel,
        grid=(t // tm,),
        in_specs=[pl.BlockSpec((tm, d), lambda i: (i, 0)),
                  pl.BlockSpec((1, d), lambda i: (0, 0)),
                  pl.BlockSpec((1, d), lambda i: (0, 0))],
        out_specs=pl.BlockSpec((tm, d), lambda i: (i, 0)),
        out_shape=jax.ShapeDtypeStruct((t, d), F32),
        compiler_params=_params(("arbitrary",), 32),
        name="ln_in",
    )(x, g.reshape(1, d), b.reshape(1, d))


def _matmul_kernel(x_ref, w_ref, o_ref):
    o_ref[...] = _dot(x_ref[...].astype(BF16), w_ref[...])


def _matmul_call(x, w, tm, tn, name):
    m, k = x.shape
    n = w.shape[1]
    return pl.pallas_call(
        _matmul_kernel,
        grid=(n // tn, m // tm),
        in_specs=[pl.BlockSpec((tm, k), lambda j, i: (i, 0)),
                  pl.BlockSpec((k, tn), lambda j, i: (0, j))],
        out_specs=pl.BlockSpec((tm, tn), lambda j, i: (i, j)),
        out_shape=jax.ShapeDtypeStruct((m, n), F32),
        compiler_params=_params(("arbitrary", "arbitrary"), 48),
        name=name,
    )(x, w)


CONV_HALO = 32
CONV_ROWS = 16


def _conv_kernel(a_ref, g_ref, w_ref, b_ref, lg_ref, lb_ref, o_ref, ubuf, shifted, *, ts, taps):
    rows = CONV_HALO + ts

    @pl.when(pl.program_id(1) == 0)
    def _():
        ubuf[0:CONV_HALO, :] = jnp.zeros((CONV_HALO, ubuf.shape[1]), F32)
        ubuf[rows:rows + SUBLANES, :] = jnp.zeros((SUBLANES, ubuf.shape[1]), F32)

    ubuf[CONV_HALO:rows, :] = a_ref[...] * _sigmoid(g_ref[...])
    for s in range(1, SUBLANES):
        shifted[s - 1] = ubuf[s:s + rows, :]
    bias = b_ref[...]
    lg = lg_ref[...]
    lb = lb_ref[...]
    for r0 in range(0, ts, CONV_ROWS):
        acc = jnp.zeros((CONV_ROWS, ubuf.shape[1]), F32) + bias
        for k in range(taps):
            off = CONV_HALO - (taps - 1) + k + r0
            s = off % SUBLANES
            base = off - s
            if s == 0:
                win = ubuf[base:base + CONV_ROWS, :]
            else:
                win = shifted[s - 1, base:base + CONV_ROWS, :]
            acc = acc + win * w_ref[k:k + 1, :]
        y = _layer_norm(acc, lg, lb)
        o_ref[r0:r0 + CONV_ROWS, :] = y * _sigmoid(y)
    ubuf[0:CONV_HALO, :] = ubuf[ts:ts + CONV_HALO, :]


def _conv_call(proj, w, b, lg, lb, batch, seq, ts):
    taps, c = w.shape
    nt = seq // ts
    t = batch * seq
    return pl.pallas_call(
        functools.partial(_conv_kernel, ts=ts, taps=taps),
        grid=(batch, nt),
        in_specs=[pl.BlockSpec((ts, c), lambda bi, j: (bi * nt + j, 0)),
                  pl.BlockSpec((ts, c), lambda bi, j: (bi * nt + j, 1)),
                  pl.BlockSpec((taps, c), lambda bi, j: (0, 0)),
                  pl.BlockSpec((1, c), lambda bi, j: (0, 0)),
                  pl.BlockSpec((1, c), lambda bi, j: (0, 0)),
                  pl.BlockSpec((1, c), lambda bi, j: (0, 0))],
        out_specs=pl.BlockSpec((ts, c), lambda bi, j: (bi * nt + j, 0)),
        out_shape=jax.ShapeDtypeStruct((t, c), F32),
        scratch_shapes=[pltpu.VMEM((CONV_HALO + ts + SUBLANES, c), F32),
                        pltpu.VMEM((SUBLANES - 1, CONV_HALO + ts, c), F32)],
        compiler_params=_params(("arbitrary", "arbitrary"), 48),
        name="conv_module",
    )(proj, proj, w, b.reshape(1, c), lg.reshape(1, c), lb.reshape(1, c))


SSD_HALO = 8


def _softplus(v):
    return jnp.maximum(v, 0.0) + jnp.log(1.0 + jnp.exp(-jnp.abs(v)))


def _ssd_kernel(z_ref, xbc_ref, dt_ref, dtp_ref, cw_ref, cb_ref, dtb_ref, alog_ref, dtbp_ref,
                alogp_ref, dskip_ref, ng_ref, exp_ref, tri_ref, bd2_ref, o_ref,
                xbuf, prev, ybuf, *, ts, taps, width, nstate):
    q = SSD_CHUNK
    gw = width // SSD_GROUPS
    pairs_per_group = gw // LANES

    @pl.when(pl.program_id(1) == 0)
    def _():
        xbuf[0:SSD_HALO, :] = jnp.zeros((SSD_HALO, xbuf.shape[1]), F32)
        prev[...] = jnp.zeros(prev.shape, F32)

    xbuf[SSD_HALO:SSD_HALO + ts, :] = xbc_ref[...]

    a_neg = -jnp.exp(alog_ref[...])
    dt = _softplus(dt_ref[...] + dtb_ref[...])
    expand = exp_ref[...]
    dtx = _dot_exact(dt, expand)
    dax = _dot_exact(dt * a_neg, expand)
    a_neg_p = -jnp.exp(alogp_ref[...])
    tri = tri_ref[...]
    bd2 = bd2_ref[...]
    lane = lax.broadcasted_iota(jnp.int32, (q, LANES), 1)
    row = lax.broadcasted_iota(jnp.int32, (q, LANES), 0)
    causal2 = row >= (lane % q)
    left = lane < q

    for c in range(ts // q):
        r0 = c * q
        xc = jnp.zeros((q, xbuf.shape[1]), F32) + cb_ref[...]
        for k in range(taps):
            off = SSD_HALO - (taps - 1) + k + r0
            xc = xc + xbuf[off:off + q, :] * cw_ref[k:k + 1, :]
        xc = xc * _sigmoid(xc)
        xs = xc[:, :width]
        bm = xc[:, width:width + SSD_GROUPS * nstate]
        cm = xc[:, width + SSD_GROUPS * nstate:]

        acx = _dot_exact(tri, dax[r0:r0 + q, :])
        dap = _softplus(dtp_ref[:, c * LANES:(c + 1) * LANES] + dtbp_ref[...]) * a_neg_p
        arow = _dot_exact(dap, bd2)
        last = acx[q - 1:q, :]
        x_dt = xs * dtx[r0:r0 + q, :]
        x_state = (x_dt * jnp.exp(last - acx)).astype(BF16)
        x_dt_b = x_dt.astype(BF16)
        exp_ac = jnp.exp(acx)
        chunk_decay = jnp.exp(last)
        skip = xs * dskip_ref[...]

        for g in range(SSD_GROUPS):
            gs = slice(g * gw, (g + 1) * gw)
            bg = bm[:, g * nstate:(g + 1) * nstate].astype(BF16)
            cg = cm[:, g * nstate:(g + 1) * nstate].astype(BF16)
            cb2 = _dot_nt(cg, jnp.concatenate([bg, bg], axis=0))
            prev_g = prev[g]
            y_off = _dot(cg, prev_g.astype(BF16)) * exp_ac[:, gs]
            states = lax.dot_general(bg, x_state[:, gs], (((0,), (0,)), ((), ())),
                                     preferred_element_type=F32)
            prev[g] = prev_g * chunk_decay[:, gs] + states
            for pr in range(pairs_per_group):
                pi = g * pairs_per_group + pr
                cols = slice(pi * LANES, (pi + 1) * LANES)
                seg = acx[:, cols] - arow[pi:pi + 1, :]
                decay = jnp.exp(jnp.where(causal2, seg, NEG_INF))
                m2 = (cb2 * decay).astype(BF16)
                xp = x_dt_b[:, cols]
                zero = jnp.zeros_like(xp)
                bdx = jnp.concatenate([jnp.where(left, xp, zero), jnp.where(left, zero, xp)], axis=0)
                y_diag = _dot(m2, bdx)
                ybuf[r0:r0 + q, cols] = (y_diag + y_off[:, pr * LANES:(pr + 1) * LANES] + skip[:, cols])

    xbuf[0:SSD_HALO, :] = xbuf[ts:ts + SSD_HALO, :]

    z = z_ref[...]
    yg = ybuf[...] * (z * _sigmoid(z))
    ng = ng_ref[...]
    for g in range(SSD_GROUPS):
        gs = slice(g * gw, (g + 1) * gw)
        v = yg[:, gs]
        ms = jnp.mean(v * v, axis=-1, keepdims=True)
        o_ref[:, gs] = v * lax.rsqrt(ms + RMS_EPS) * ng[:, gs]


def _ssd_call(proj, dt_raw, dtp, cw, cb, dt_bias, a_log, d_skip, norm_g, batch, seq, ts, conv_width):
    taps, cx = cw.shape
    width = norm_g.shape[0]
    heads = dt_bias.shape[0]
    nstate = (cx - width) // (2 * SSD_GROUPS)
    npairs = heads // 2
    q = SSD_CHUNK
    nt = seq // ts
    t = batch * seq
    z_blk = (2 * conv_width) // width
    x_blk = (2 * conv_width + width) // cx
    assert z_blk * width == 2 * conv_width and x_blk * cx == 2 * conv_width + width
    pad = LANES - heads
    dtb = jnp.pad(dt_bias, (0, pad)).reshape(1, LANES)
    alog = jnp.pad(a_log, (0, pad)).reshape(1, LANES)
    dtbp = jnp.repeat(dt_bias.reshape(npairs, 2), q, axis=1)
    alogp = jnp.repeat(a_log.reshape(npairs, 2), q, axis=1)
    dskip = jnp.repeat(d_skip, SSD_HEAD_DIM).reshape(1, width)
    head_of_col = jnp.arange(width) // SSD_HEAD_DIM
    expand = (jnp.arange(LANES)[:, None] == head_of_col[None, :]).astype(F32)
    ii = jnp.arange(q)
    tri = (ii[:, None] >= ii[None, :]).astype(F32)
    jj = jnp.arange(LANES)
    bd2 = ((jj[:, None] // q == jj[None, :] // q) & (jj[:, None] % q <= jj[None, :] % q)).astype(F32)
    const = lambda bi, j: (0, 0)
    return pl.pallas_call(
        functools.partial(_ssd_kernel, ts=ts, taps=taps, width=width, nstate=nstate),
        grid=(batch, nt),
        in_specs=[pl.BlockSpec((ts, width), lambda bi, j: (bi * nt + j, z_blk)),
                  pl.BlockSpec((ts, cx), lambda bi, j: (bi * nt + j, x_blk)),
                  pl.BlockSpec((ts, LANES), lambda bi, j: (bi * nt + j, 0)),
                  pl.BlockSpec((npairs, 2 * ts), lambda bi, j: (0, bi * nt + j)),
                  pl.BlockSpec((taps, cx), const),
                  pl.BlockSpec((1, cx), const),
                  pl.BlockSpec((1, LANES), const),
                  pl.BlockSpec((1, LANES), const),
                  pl.BlockSpec((npairs, LANES), const),
                  pl.BlockSpec((npairs, LANES), const),
                  pl.BlockSpec((1, width), const),
                  pl.BlockSpec((1, width), const),
                  pl.BlockSpec((LANES, width), const),
                  pl.BlockSpec((q, q), const),
                  pl.BlockSpec((LANES, LANES), const)],
        out_specs=pl.BlockSpec((ts, width), lambda bi, j: (bi * nt + j, 0)),
        out_shape=jax.ShapeDtypeStruct((t, width), F32),
        scratch_shapes=[pltpu.VMEM((SSD_HALO + ts, cx), F32),
                        pltpu.VMEM((SSD_GROUPS, nstate, width // SSD_GROUPS), F32),
                        pltpu.VMEM((ts, width), F32)],
        compiler_params=_params(("arbitrary", "arbitrary"), 48),
        name="ssd_mixer",
    )(proj, proj, dt_raw, dtp, cw, cb.reshape(1, cx), dtb, alog, dtbp, alogp, dskip,
      norm_g.reshape(1, width), expand, tri, bd2)


def _outproj_kernel(yc_ref, ys_ref, h_ref, w1_ref, w2_ref, g_ref, b_ref, o_ref, *, alpha):
    mix = _dot(yc_ref[...].astype(BF16), w1_ref[...]) + _dot(ys_ref[...].astype(BF16), w2_ref[...])
    o_ref[...] = _layer_norm(alpha * h_ref[...] + mix, g_ref[...], b_ref[...])


def _outproj_call(y_conv, y_ssd, h, w1, w2, g, b, alpha, tm):
    t, d = h.shape
    c1 = y_conv.shape[1]
    c2 = y_ssd.shape[1]
    const = lambda i: (0, 0)
    return pl.pallas_call(
        functools.partial(_outproj_kernel, alpha=alpha),
        grid=(t // tm,),
        in_specs=[pl.BlockSpec((tm, c1), lambda i: (i, 0)),
                  pl.BlockSpec((tm, c2), lambda i: (i, 0)),
                  pl.BlockSpec((tm, d), lambda i: (i, 0)),
                  pl.BlockSpec((c1, d), const),
                  pl.BlockSpec((c2, d), const),
                  pl.BlockSpec((1, d), const),
                  pl.BlockSpec((1, d), const)],
        out_specs=pl.BlockSpec((tm, d), lambda i: (i, 0)),
        out_shape=jax.ShapeDtypeStruct((t, d), F32),
        compiler_params=_params(("arbitrary",), 48),
        name="outproj_ln",
    )(y_conv, y_ssd, h, w1, w2, g.reshape(1, d), b.reshape(1, d))


def _scores_kernel(h_ref, wq_ref, keys_ref, s_ref, *, nsides, half):
    q = _dot(h_ref[...].astype(BF16), wq_ref[...])
    for hs in range(nsides):
        qh = q[:, hs * half:(hs + 1) * half].astype(BF16)
        s_ref[hs] = _dot_nt(keys_ref[hs], qh)


def _scores_call(h, wq, keys, tm):
    t, d = h.shape
    heads, _, nkeys, half = keys.shape
    qd = wq.shape[1]
    keys2 = keys.reshape(heads * 2, nkeys, half).astype(BF16)
    return pl.pallas_call(
        functools.partial(_scores_kernel, nsides=heads * 2, half=half),
        grid=(t // tm,),
        in_specs=[pl.BlockSpec((tm, d), lambda i: (i, 0)),
                  pl.BlockSpec((d, qd), lambda i: (0, 0)),
                  pl.BlockSpec((heads * 2, nkeys, half), lambda i: (0, 0, 0))],
        out_specs=pl.BlockSpec((heads * 2, nkeys, tm), lambda i: (0, 0, i)),
        out_shape=jax.ShapeDtypeStruct((heads * 2, nkeys, t), F32),
        compiler_params=_params(("arbitrary",), 48),
        name="peer_scores",
    )(h, wq, keys2)


TOPK_HEADS = 2


def _topk_scratch(nkeys):
    k = PEER_TOPK
    return [pltpu.VMEM((nkeys, LANES), F32),
            pltpu.VMEM((5 * k, LANES), F32),
            pltpu.VMEM((2, k, LANES), F32),
            pltpu.VMEM((2, k, LANES), F32),
            pltpu.VMEM((k, LANES), F32),
            pltpu.VMEM((k, LANES), F32)]


def _topk_work(s_ref, idx_ref, gate_ref, scratch, nkeys):
    s_scr, c_scr, ts_scr, ti_scr, bs_scr, bj_scr = scratch
    k = PEER_TOPK
    iota1 = lax.broadcasted_iota(jnp.int32, (nkeys, LANES), 0).astype(F32)
    row8 = lax.broadcasted_iota(jnp.int32, (SUBLANES, LANES), 0)
    flat = [lax.broadcasted_iota(jnp.int32, (k, LANES), 0)]
    flat += [k1 * k + row8 for k1 in range(1, SUBLANES)]
    flat.append((SUBLANES + row8) * k)
    flat = jnp.concatenate(flat, axis=0).astype(F32)

    def select_round(src, ids, nvals, r, out_s, out_i):
        s = src[...]
        m = jnp.max(s, axis=0, keepdims=True)
        idx = jnp.min(jnp.where(s == m, ids, float(nvals)), axis=0, keepdims=True)
        out_s[r:r + 1, :] = m
        out_i[r:r + 1, :] = idx
        src[...] = jnp.where(ids == idx, NEG_INF, s)

    for hd in range(TOPK_HEADS):
        for side in range(2):
            s_scr[...] = s_ref[hd * 2 + side]
            for r in range(k):
                select_round(s_scr, iota1, nkeys, r, ts_scr.at[side], ti_scr.at[side])
                yield
        s0 = ts_scr[0]
        s1 = ts_scr[1]
        cand = [s0[0:1, :] + s1]
        for k1 in range(1, SUBLANES):
            v = s0[k1:k1 + 1, :] + s1[0:SUBLANES, :]
            cand.append(jnp.where(row8 < k // (k1 + 1), v, NEG_INF))
        cand.append(s0[SUBLANES:, :] + s1[0:1, :])
        c_scr[...] = jnp.concatenate(cand, axis=0)
        yield
        for r in range(k):
            select_round(c_scr, flat, k * k, r, bs_scr, bj_scr)
            yield
        best_s = bs_scr[...]
        best_j = bj_scr[...].astype(jnp.int32)
        k1 = best_j // k
        k2 = best_j % k
        i0 = ti_scr[0]
        i1 = ti_scr[1]
        e1 = jnp.zeros((k, LANES), F32)
        e2 = jnp.zeros((k, LANES), F32)
        for kk in range(k):
            e1 = e1 + jnp.where(k1 == kk, i0[kk:kk + 1, :], 0.0)
            e2 = e2 + jnp.where(k2 == kk, i1[kk:kk + 1, :], 0.0)
        ex = jnp.exp(best_s - best_s[0:1, :])
        idx_ref[hd * k:(hd + 1) * k, :] = (e1 * nkeys + e2).astype(jnp.int32)
        gate_ref[hd * k:(hd + 1) * k, :] = ex / jnp.sum(ex, axis=0, keepdims=True)
        yield


def _topk_kernel(s_ref, idx_ref, gate_ref, *scratch, nkeys):
    for _ in _topk_work(s_ref, idx_ref, gate_ref, scratch, nkeys):
        pass


def _topk_specs(heads, nkeys, group0):
    k = PEER_TOPK
    units = heads // TOPK_HEADS
    s_spec = pl.BlockSpec((2 * TOPK_HEADS, nkeys, LANES), lambda j: (j % units, 0, group0 + j // units))
    o_spec = pl.BlockSpec((TOPK_HEADS * k, LANES), lambda j: (j % units, j // units))
    return units, s_spec, o_spec


def _topk_call(scores, token0, ntok):
    nsides, nkeys, _ = scores.shape
    heads = nsides // 2
    k = PEER_TOPK
    assert k == 2 * SUBLANES
    units, s_spec, o_spec = _topk_specs(heads, nkeys, token0 // LANES)
    return pl.pallas_call(
        functools.partial(_topk_kernel, nkeys=nkeys),
        grid=(units * (ntok // LANES),),
        in_specs=[s_spec],
        out_specs=[o_spec, o_spec],
        out_shape=[jax.ShapeDtypeStruct((heads * k, ntok), jnp.int32),
                   jax.ShapeDtypeStruct((heads * k, ntok), F32)],
        scratch_shapes=_topk_scratch(nkeys),
        compiler_params=_params(("arbitrary",), 32),
        name="peer_topk",
    )(scores)


def _gelu(v):
    return 0.5 * v * (1.0 + lax.erf(v * (2.0 ** -0.5)))


def _pack_expert_tables(u_tab, v_tab):
    e, d = u_tab.shape
    half = d // 2

    def pack(tab):
        bits = lax.bitcast_convert_type(tab.astype(BF16), jnp.uint16).astype(jnp.uint32)
        words = bits[:, :half] | (bits[:, half:] << 16)
        return lax.bitcast_convert_type(words, jnp.int32).reshape(e, half // LANES, LANES)

    return jnp.concatenate([pack(u_tab), pack(v_tab)], axis=1)


def _unpack_words(w):
    lo = lax.bitcast_convert_type(w << 16, F32)
    hi = lax.bitcast_convert_type(w & jnp.int32(-65536), F32)
    return lo, hi


ROUTE_PIECES = 4
GATHER_BUFFERS = 4
GATHER_AHEAD = 2


def _gather_kernel(idx_cur_ref, idx_nxt_ref, x_ref, g_ref, *refs, tb, ne, d, route_nkeys):
    if route_nkeys:
        s_ref, tab_hbm, o_ref, idx_o_ref, gate_o_ref = refs[:5]
        scratch = refs[5:]
    else:
        tab_hbm, o_ref = refs[:2]
        scratch = refs[2:]
    bufs = scratch[:GATHER_BUFFERS]
    w_scr, sem = scratch[GATHER_BUFFERS:GATHER_BUFFERS + 2]
    side_work = None
    if route_nkeys:
        side_work = _topk_work(s_ref, idx_o_ref, gate_o_ref, scratch[GATHER_BUFFERS + 2:], route_nkeys)
    side_items = TOPK_HEADS * (3 * PEER_TOPK + 2)
    side_points = GATHER_BUFFERS * 2 * tb
    progress = {"points": 0, "items": 0}

    def side_step():
        if side_work is None:
            return
        progress["points"] += 1
        target = -(-side_items * progress["points"] // side_points)
        while progress["items"] < target:
            next(side_work, None)
            progress["items"] += 1

    j = pl.program_id(0)
    nsteps = pl.num_programs(0)
    groups = ne // SUBLANES
    nc = d // (2 * LANES)

    def row_copy(idx_ref, row, t, k, dst):
        e = idx_ref[row, k]
        return pltpu.make_async_copy(tab_hbm.at[e],
                                     bufs[dst].at[t * groups + k // SUBLANES, :, k % SUBLANES, :],
                                     sem.at[dst])

    def block_wait(src):
        pltpu.make_async_copy(bufs[src], bufs[src], sem.at[src]).wait()

    @pl.when(j == 0)
    def _():
        def body(t, carry):
            for b in range(GATHER_AHEAD):
                for k in range(ne):
                    row_copy(idx_cur_ref, b * tb + t, t, k, b).start()
            return carry
        lax.fori_loop(0, tb, body, 0)

    def block(par):
        buf = bufs[par]
        block_wait(par)
        ahead = par + GATHER_AHEAD
        nxt_ref = idx_cur_ref if ahead < GATHER_BUFFERS else idx_nxt_ref
        nxt_row0 = (ahead % GATHER_BUFFERS) * tb
        dst = ahead % GATHER_BUFFERS
        fetches = [(t, k) for t in range(tb) for k in range(ne)]
        per_phase = len(fetches) // (2 * tb)

        def fetch_some(phase):
            for t, k in fetches[phase * per_phase:(phase + 1) * per_phase]:
                row_copy(nxt_ref, nxt_row0 + t, t, k, dst).start(priority=k % 2)

        cols = []
        for t in range(tb):
            x = x_ref[par * tb + t:par * tb + t + 1, :]
            xs = [x[:, c * LANES:(c + 1) * LANES] for c in range(2 * nc)]
            parts = []
            for g in range(groups):
                acc = None
                for c in range(nc):
                    lo, hi = _unpack_words(buf[t * groups + g, c])
                    term = lo * xs[c] + hi * xs[nc + c]
                    acc = term if acc is None else acc + term
                parts.append(acc)
            cols.append(jnp.sum(jnp.concatenate(parts, axis=0), axis=-1, keepdims=True))
            fetch_some(t)
            side_step()
        act = jnp.concatenate(cols, axis=1)
        w_scr[par] = g_ref[par] * _gelu(act)
        for t in range(tb):
            acc_lo = [None] * nc
            acc_hi = [None] * nc
            for g in range(groups):
                wb = jnp.broadcast_to(w_scr[par, g * SUBLANES:(g + 1) * SUBLANES, t:t + 1],
                                      (SUBLANES, LANES))
                for c in range(nc):
                    lo, hi = _unpack_words(buf[t * groups + g, nc + c])
                    acc_lo[c] = lo * wb if acc_lo[c] is None else acc_lo[c] + lo * wb
                    acc_hi[c] = hi * wb if acc_hi[c] is None else acc_hi[c] + hi * wb
            o_ref[par * tb + t:par * tb + t + 1, :] = jnp.concatenate(
                [jnp.sum(a, axis=0, keepdims=True) for a in acc_lo + acc_hi], axis=1)
            fetch_some(tb + t)
            side_step()

    for par in range(GATHER_BUFFERS):
        block(par)
    if side_work is not None:
        for _ in side_work:
            pass

    @pl.when(j == nsteps - 1)
    def _():
        for b in range(GATHER_AHEAD):
            block_wait(b)


def _gather_call(h, idx, gate3, table, tb, token0=0, scores=None, route_token0=0):
    d = h.shape[1]
    ntok, ne = idx.shape
    rows = GATHER_BUFFERS * tb
    ns = ntok // rows
    step0 = token0 // rows
    nrows = table.shape[1]
    in_specs = [pl.BlockSpec((rows, ne), lambda j: (j, 0), memory_space=pltpu.SMEM),
                pl.BlockSpec((rows, ne), lambda j: (jnp.minimum(j + 1, ns - 1), 0),
                             memory_space=pltpu.SMEM),
                pl.BlockSpec((rows, d), lambda j: (step0 + j, 0)),
                pl.BlockSpec((GATHER_BUFFERS, ne, tb), lambda j: (j, 0, 0))]
    out_specs = [pl.BlockSpec((rows, d), lambda j: (j, 0))]
    out_shape = [jax.ShapeDtypeStruct((ntok, d), F32)]
    scratch = [pltpu.VMEM((tb * ne // SUBLANES, nrows, SUBLANES, LANES), jnp.int32)
               for _ in range(GATHER_BUFFERS)]
    scratch += [pltpu.VMEM((GATHER_BUFFERS, ne, tb), F32), pltpu.SemaphoreType.DMA((GATHER_BUFFERS,))]
    args = [idx, idx, h, gate3]
    route_nkeys = 0
    if scores is not None:
        nsides, route_nkeys, _ = scores.shape
        units, s_spec, o_spec = _topk_specs(nsides // 2, route_nkeys, route_token0 // LANES)
        assert ns == units * (ntok // LANES)
        in_specs.append(s_spec)
        out_specs += [o_spec, o_spec]
        out_shape += [jax.ShapeDtypeStruct((ne, ntok), jnp.int32), jax.ShapeDtypeStruct((ne, ntok), F32)]
        scratch += _topk_scratch(route_nkeys)
        args.append(scores)
    in_specs.append(pl.BlockSpec(memory_space=pl.ANY))
    args.append(table)
    out = pl.pallas_call(
        functools.partial(_gather_kernel, tb=tb, ne=ne, d=d, route_nkeys=route_nkeys),
        grid=(ns,),
        in_specs=in_specs,
        out_specs=out_specs,
        out_shape=out_shape,
        scratch_shapes=scratch,
        compiler_params=_params(("arbitrary",), 56),
        name="peer_gather_route" if route_nkeys else "peer_gather",
    )(*args)
    return out if route_nkeys else out[0]


def _ple_kernel(h_ref, f_ref, p_ref, wg_ref, wp_ref, g_ref, b_ref, o_ref, *, alpha):
    h = h_ref[...]
    gate = _sigmoid(_dot(h.astype(BF16), wg_ref[...]))
    emb = _dot(p_ref[...].astype(BF16), wp_ref[...])
    o_ref[...] = _layer_norm(alpha * h + f_ref[...] + gate * emb, g_ref[...], b_ref[...])


def _ple_call(h, ffn, p, wg, wp, g, b, alpha, tm):
    t, d = h.shape
    pd = p.shape[1]
    const = lambda i: (0, 0)
    return pl.pallas_call(
        functools.partial(_ple_kernel, alpha=alpha),
        grid=(t // tm,),
        in_specs=[pl.BlockSpec((tm, d), lambda i: (i, 0)),
                  pl.BlockSpec((tm, d), lambda i: (i, 0)),
                  pl.BlockSpec((tm, pd), lambda i: (i, 0)),
                  pl.BlockSpec((d, d), const),
                  pl.BlockSpec((pd, d), const),
                  pl.BlockSpec((1, d), const),
                  pl.BlockSpec((1, d), const)],
        out_specs=pl.BlockSpec((tm, d), lambda i: (i, 0)),
        out_shape=jax.ShapeDtypeStruct((t, d), F32),
        compiler_params=_params(("arbitrary",), 48),
        name="ple_ln",
    )(h, ffn, p, wg, wp, g.reshape(1, d), b.reshape(1, d))


def _tile(n, want):
    return want if n % want == 0 else n


def kernel(x, p, ln_in_g, ln_in_b, w_in, conv_w, conv_b, conv_ln_g, conv_ln_b, ssd_conv_w, ssd_conv_b, dt_bias, a_log, d_skip, ssd_norm_g, w_out, ln1_g, ln1_b, peer_wq, peer_keys, peer_u, peer_v, ple_w_gate, ple_w_proj, ln2_g, ln2_b):
    batch, seq, d = x.shape
    depth = w_in.shape[0]
    t = batch * seq
    conv_width = conv_w.shape[2]
    width = ssd_norm_g.shape[1]
    cx = ssd_conv_w.shape[2]
    heads = dt_bias.shape[1]
    n_main = 2 * conv_width + width + cx
    alpha = float((2 * depth) ** 0.25)
    q = SSD_CHUNK

    tm = _tile(t, 256)
    ts = _tile(seq, 256)
    tb = 8
    tn_main = n_main // 3 if (n_main // 3) % LANES == 0 and n_main % 3 == 0 else n_main

    h = _ln_call(x.reshape(t, d), ln_in_g, ln_in_b, tm)
    p2 = p.reshape(depth, t, p.shape[-1])
    for i in range(depth):
        w_main = w_in[i, :, :n_main].astype(BF16)
        w_dt = jnp.pad(w_in[i, :, n_main:], ((0, 0), (0, LANES - heads))).astype(BF16)
        proj = _matmul_call(h, w_main, _tile(t, 512), tn_main, "in_proj")
        dt_raw = _matmul_call(h, w_dt, _tile(t, 512), LANES, "dt_proj")
        dtp = (dt_raw[:, :heads].reshape(t // q, q, heads // 2, 2)
               .transpose(2, 0, 3, 1).reshape(heads // 2, 2 * t))
        y_conv = _conv_call(proj, conv_w[i], conv_b[i], conv_ln_g[i], conv_ln_b[i], batch, seq, ts)
        y_ssd = _ssd_call(proj, dt_raw, dtp, ssd_conv_w[i], ssd_conv_b[i], dt_bias[i], a_log[i],
                          d_skip[i], ssd_norm_g[i], batch, seq, ts, conv_width)
        w_o = w_out[i].astype(BF16)
        h = _outproj_call(y_conv, y_ssd, h, w_o[:conv_width], w_o[conv_width:], ln1_g[i], ln1_b[i],
                          alpha, tm)
        scores = _scores_call(h, peer_wq[i].astype(BF16), peer_keys[i], tm)
        table = _pack_expert_tables(peer_u[i], peer_v[i])
        tp = t // ROUTE_PIECES
        idx_t, gate_t = _topk_call(scores, 0, tp)
        ffn = []
        for piece in range(ROUTE_PIECES):
            ne = idx_t.shape[0]
            idx = idx_t.T
            gate3 = gate_t.reshape(ne, tp // tb, tb).transpose(1, 0, 2)
            if piece + 1 < ROUTE_PIECES:
                part, idx_t, gate_t = _gather_call(h, idx, gate3, table, tb, piece * tp,
                                                   scores, (piece + 1) * tp)
            else:
                part = _gather_call(h, idx, gate3, table, tb, piece * tp)
            ffn.append(part)
        ffn = jnp.concatenate(ffn, axis=0)
        h = _ple_call(h, ffn, p2[i], ple_w_gate[i].astype(BF16), ple_w_proj[i].astype(BF16),
                      ln2_g[i], ln2_b[i], alpha, tm)
    return h.reshape(batch, seq, d)
```

```python
import functools

import jax
import jax.numpy as jnp
from jax import lax
from jax.experimental import pallas as pl
from jax.experimental.pallas import tpu as pltpu

LN_EPS = 1e-5
RMS_EPS = 1e-5
SSD_GROUPS = 2
SSD_HEAD_DIM = 64
SSD_CHUNK = 64
PEER_TOPK = 16
LANES = 128
SUBLANES = 8
F32 = jnp.float32
BF16 = jnp.bfloat16
HIGHEST = lax.Precision.HIGHEST
NEG_INF = float("-inf")
MIB = 1024 * 1024


def _params(sem, vmem_mib):
    return pltpu.CompilerParams(dimension_semantics=sem, vmem_limit_bytes=vmem_mib * MIB)


def _layer_norm(v, g, b):
    mu = jnp.mean(v, axis=-1, keepdims=True)
    d = v - mu
    var = jnp.mean(d * d, axis=-1, keepdims=True)
    return d * lax.rsqrt(var + LN_EPS) * g + b


def _sigmoid(v):
    return 1.0 / (1.0 + jnp.exp(-v))


def _dot(a, b):
    return jnp.dot(a, b, preferred_element_type=F32)


def _dot_nt(a, b):
    return lax.dot_general(a, b, (((1,), (1,)), ((), ())), preferred_element_type=F32)


def _dot_exact(a, b):
    return jnp.dot(a, b, preferred_element_type=F32, precision=HIGHEST)


def _ln_kernel(x_ref, g_ref, b_ref, o_ref):
    o_ref[...] = _layer_norm(x_ref[...], g_ref[...], b_ref[...])


def _ln_call(x, g, b, tm):
    t, d = x.shape
    return pl.pallas_call(
        _ln_kernel,
        grid=(t // tm,),
        in_specs=[pl.BlockSpec((tm, d), lambda i: (i, 0)),
                  pl.BlockSpec((1, d), lambda i: (0, 0)),
                  pl.BlockSpec((1, d), lambda i: (0, 0))],
        out_specs=pl.BlockSpec((tm, d), lambda i: (i, 0)),
        out_shape=jax.ShapeDtypeStruct((t, d), F32),
        compiler_params=_params(("arbitrary",), 32),
        name="ln_in",
    )(x, g.reshape(1, d), b.reshape(1, d))


def _matmul_kernel(x_ref, w_ref, o_ref):
    o_ref[...] = _dot(x_ref[...].astype(BF16), w_ref[...])


def _matmul_call(x, w, tm, tn, name):
    m, k = x.shape
    n = w.shape[1]
    return pl.pallas_call(
        _matmul_kernel,
        grid=(n // tn, m // tm),
        in_specs=[pl.BlockSpec((tm, k), lambda j, i: (i, 0)),
                  pl.BlockSpec((k, tn), lambda j, i: (0, j))],
        out_specs=pl.BlockSpec((tm, tn), lambda j, i: (i, j)),
        out_shape=jax.ShapeDtypeStruct((m, n), F32),
        compiler_params=_params(("arbitrary", "arbitrary"), 48),
        name=name,
    )(x, w)


CONV_HALO = 32
CONV_ROWS = 16


def _conv_kernel(a_ref, g_ref, w_ref, b_ref, lg_ref, lb_ref, o_ref, ubuf, shifted, *, ts, taps):
    rows = CONV_HALO + ts

    @pl.when(pl.program_id(1) == 0)
    def _():
        ubuf[0:CONV_HALO, :] = jnp.zeros((CONV_HALO, ubuf.shape[1]), F32)
        ubuf[rows:rows + SUBLANES, :] = jnp.zeros((SUBLANES, ubuf.shape[1]), F32)

    ubuf[CONV_HALO:rows, :] = a_ref[...] * _sigmoid(g_ref[...])
    for s in range(1, SUBLANES):
        shifted[s - 1] = ubuf[s:s + rows, :]
    bias = b_ref[...]
    lg = lg_ref[...]
    lb = lb_ref[...]
    for r0 in range(0, ts, CONV_ROWS):
        acc = jnp.zeros((CONV_ROWS, ubuf.shape[1]), F32) + bias
        for k in range(taps):
            off = CONV_HALO - (taps - 1) + k + r0
            s = off % SUBLANES
            base = off - s
            if s == 0:
                win = ubuf[base:base + CONV_ROWS, :]
            else:
                win = shifted[s - 1, base:base + CONV_ROWS, :]
            acc = acc + win * w_ref[k:k + 1, :]
        y = _layer_norm(acc, lg, lb)
        o_ref[r0:r0 + CONV_ROWS, :] = y * _sigmoid(y)
    ubuf[0:CONV_HALO, :] = ubuf[ts:ts + CONV_HALO, :]


def _conv_call(proj, w, b, lg, lb, batch, seq, ts):
    taps, c = w.shape
    nt = seq // ts
    t = batch * seq
    return pl.pallas_call(
        functools.partial(_conv_kernel, ts=ts, taps=taps),
        grid=(batch, nt),
        in_specs=[pl.BlockSpec((ts, c), lambda bi, j: (bi * nt + j, 0)),
                  pl.BlockSpec((ts, c), lambda bi, j: (bi * nt + j, 1)),
                  pl.BlockSpec((taps, c), lambda bi, j: (0, 0)),
                  pl.BlockSpec((1, c), lambda bi, j: (0, 0)),
                  pl.BlockSpec((1, c), lambda bi, j: (0, 0)),
                  pl.BlockSpec((1, c), lambda bi, j: (0, 0))],
        out_specs=pl.BlockSpec((ts, c), lambda bi, j: (bi * nt + j, 0)),
        out_shape=jax.ShapeDtypeStruct((t, c), F32),
        scratch_shapes=[pltpu.VMEM((CONV_HALO + ts + SUBLANES, c), F32),
                        pltpu.VMEM((SUBLANES - 1, CONV_HALO + ts, c), F32)],
        compiler_params=_params(("arbitrary", "arbitrary"), 48),
        name="conv_module",
    )(proj, proj, w, b.reshape(1, c), lg.reshape(1, c), lb.reshape(1, c))


SSD_HALO = 8


def _softplus(v):
    return jnp.maximum(v, 0.0) + jnp.log(1.0 + jnp.exp(-jnp.abs(v)))


def _ssd_kernel(z_ref, xbc_ref, dt_ref, dtp_ref, cw_ref, cb_ref, dtb_ref, alog_ref, dtbp_ref,
                alogp_ref, dskip_ref, ng_ref, exp_ref, tri_ref, bd2_ref, o_ref,
                xbuf, prev, ybuf, *, ts, taps, width, nstate):
    q = SSD_CHUNK
    gw = width // SSD_GROUPS
    pairs_per_group = gw // LANES

    @pl.when(pl.program_id(1) == 0)
    def _():
        xbuf[0:SSD_HALO, :] = jnp.zeros((SSD_HALO, xbuf.shape[1]), F32)
        prev[...] = jnp.zeros(prev.shape, F32)

    xbuf[SSD_HALO:SSD_HALO + ts, :] = xbc_ref[...]

    a_neg = -jnp.exp(alog_ref[...])
    dt = _softplus(dt_ref[...] + dtb_ref[...])
    expand = exp_ref[...]
    dtx = _dot_exact(dt, expand)
    dax = _dot_exact(dt * a_neg, expand)
    a_neg_p = -jnp.exp(alogp_ref[...])
    tri = tri_ref[...]
    bd2 = bd2_ref[...]
    lane = lax.broadcasted_iota(jnp.int32, (q, LANES), 1)
    row = lax.broadcasted_iota(jnp.int32, (q, LANES), 0)
    causal2 = row >= (lane % q)
    left = lane < q

    for c in range(ts // q):
        r0 = c * q
        xc = jnp.zeros((q, xbuf.shape[1]), F32) + cb_ref[...]
        for k in range(taps):
            off = SSD_HALO - (taps - 1) + k + r0
            xc = xc + xbuf[off:off + q, :] * cw_ref[k:k + 1, :]
        xc = xc * _sigmoid(xc)
        xs = xc[:, :width]
        bm = xc[:, width:width + SSD_GROUPS * nstate]
        cm = xc[:, width + SSD_GROUPS * nstate:]

        acx = _dot_exact(tri, dax[r0:r0 + q, :])
        dap = _softplus(dtp_ref[:, c * LANES:(c + 1) * LANES] + dtbp_ref[...]) * a_neg_p
        arow = _dot_exact(dap, bd2)
        last = acx[q - 1:q, :]
        x_dt = xs * dtx[r0:r0 + q, :]
        x_state = (x_dt * jnp.exp(last - acx)).astype(BF16)
        x_dt_b = x_dt.astype(BF16)
        exp_ac = jnp.exp(acx)
        chunk_decay = jnp.exp(last)
        skip = xs * dskip_ref[...]

        for g in range(SSD_GROUPS):
            gs = slice(g * gw, (g + 1) * gw)
            bg = bm[:, g * nstate:(g + 1) * nstate].astype(BF16)
            cg = cm[:, g * nstate:(g + 1) * nstate].astype(BF16)
            cb2 = _dot_nt(cg, jnp.concatenate([bg, bg], axis=0))
            prev_g = prev[g]
            y_off = _dot(cg, prev_g.astype(BF16)) * exp_ac[:, gs]
            states = lax.dot_general(bg, x_state[:, gs], (((0,), (0,)), ((), ())),
                                     preferred_element_type=F32)
            prev[g] = prev_g * chunk_decay[:, gs] + states
            for pr in range(pairs_per_group):
                pi = g * pairs_per_group + pr
                cols = slice(pi * LANES, (pi + 1) * LANES)
                seg = acx[:, cols] - arow[pi:pi + 1, :]
                decay = jnp.exp(jnp.where(causal2, seg, NEG_INF))
                m2 = (cb2 * decay).astype(BF16)
                xp = x_dt_b[:, cols]
                zero = jnp.zeros_like(xp)
                bdx = jnp.concatenate([jnp.where(left, xp, zero), jnp.where(left, zero, xp)], axis=0)
                y_diag = _dot(m2, bdx)
                ybuf[r0:r0 + q, cols] = (y_diag + y_off[:, pr * LANES:(pr + 1) * LANES] + skip[:, cols])

    xbuf[0:SSD_HALO, :] = xbuf[ts:ts + SSD_HALO, :]

    z = z_ref[...]
    yg = ybuf[...] * (z * _sigmoid(z))
    ng = ng_ref[...]
    for g in range(SSD_GROUPS):
        gs = slice(g * gw, (g + 1) * gw)
        v = yg[:, gs]
        ms = jnp.mean(v * v, axis=-1, keepdims=True)
        o_ref[:, gs] = v * lax.rsqrt(ms + RMS_EPS) * ng[:, gs]


def _ssd_call(proj, dt_raw, dtp, cw, cb, dt_bias, a_log, d_skip, norm_g, batch, seq, ts, conv_width):
    taps, cx = cw.shape
    width = norm_g.shape[0]
    heads = dt_bias.shape[0]
    nstate = (cx - width) // (2 * SSD_GROUPS)
    npairs = heads // 2
    q = SSD_CHUNK
    nt = seq // ts
    t = batch * seq
    z_blk = (2 * conv_width) // width
    x_blk = (2 * conv_width + width) // cx
    assert z_blk * width == 2 * conv_width and x_blk * cx == 2 * conv_width + width
    pad = LANES - heads
    dtb = jnp.pad(dt_bias, (0, pad)).reshape(1, LANES)
    alog = jnp.pad(a_log, (0, pad)).reshape(1, LANES)
    dtbp = jnp.repeat(dt_bias.reshape(npairs, 2), q, axis=1)
    alogp = jnp.repeat(a_log.reshape(npairs, 2), q, axis=1)
    dskip = jnp.repeat(d_skip, SSD_HEAD_DIM).reshape(1, width)
    head_of_col = jnp.arange(width) // SSD_HEAD_DIM
    expand = (jnp.arange(LANES)[:, None] == head_of_col[None, :]).astype(F32)
    ii = jnp.arange(q)
    tri = (ii[:, None] >= ii[None, :]).astype(F32)
    jj = jnp.arange(LANES)
    bd2 = ((jj[:, None] // q == jj[None, :] // q) & (jj[:, None] % q <= jj[None, :] % q)).astype(F32)
    const = lambda bi, j: (0, 0)
    return pl.pallas_call(
        functools.partial(_ssd_kernel, ts=ts, taps=taps, width=width, nstate=nstate),
        grid=(batch, nt),
        in_specs=[pl.BlockSpec((ts, width), lambda bi, j: (bi * nt + j, z_blk)),
                  pl.BlockSpec((ts, cx), lambda bi, j: (bi * nt + j, x_blk)),
                  pl.BlockSpec((ts, LANES), lambda bi, j: (bi * nt + j, 0)),
                  pl.BlockSpec((npairs, 2 * ts), lambda bi, j: (0, bi * nt + j)),
                  pl.BlockSpec((taps, cx), const),
                  pl.BlockSpec((1, cx), const),
                  pl.BlockSpec((1, LANES), const),
                  pl.BlockSpec((1, LANES), const),
                  pl.BlockSpec((npairs, LANES), const),
                  pl.BlockSpec((npairs, LANES), const),
                  pl.BlockSpec((1, width), const),
                  pl.BlockSpec((1, width), const),
                  pl.BlockSpec((LANES, width), const),
                  pl.BlockSpec((q, q), const),
                  pl.BlockSpec((LANES, LANES), const)],
        out_specs=pl.BlockSpec((ts, width), lambda bi, j: (bi * nt + j, 0)),
        out_shape=jax.ShapeDtypeStruct((t, width), F32),
        scratch_shapes=[pltpu.VMEM((SSD_HALO + ts, cx), F32),
                        pltpu.VMEM((SSD_GROUPS, nstate, width // SSD_GROUPS), F32),
                        pltpu.VMEM((ts, width), F32)],
        compiler_params=_params(("arbitrary", "arbitrary"), 48),
        name="ssd_mixer",
    )(proj, proj, dt_raw, dtp, cw, cb.reshape(1, cx), dtb, alog, dtbp, alogp, dskip,
      norm_g.reshape(1, width), expand, tri, bd2)


def _outproj_kernel(yc_ref, ys_ref, h_ref, w1_ref, w2_ref, g_ref, b_ref, o_ref, *, alpha):
    mix = _dot(yc_ref[...].astype(BF16), w1_ref[...]) + _dot(ys_ref[...].astype(BF16), w2_ref[...])
    o_ref[...] = _layer_norm(alpha * h_ref[...] + mix, g_ref[...], b_ref[...])


def _outproj_call(y_conv, y_ssd, h, w1, w2, g, b, alpha, tm):
    t, d = h.shape
    c1 = y_conv.shape[1]
    c2 = y_ssd.shape[1]
    const = lambda i: (0, 0)
    return pl.pallas_call(
        functools.partial(_outproj_kernel, alpha=alpha),
        grid=(t // tm,),
        in_specs=[pl.BlockSpec((tm, c1), lambda i: (i, 0)),
                  pl.BlockSpec((tm, c2), lambda i: (i, 0)),
                  pl.BlockSpec((tm, d), lambda i: (i, 0)),
                  pl.BlockSpec((c1, d), const),
                  pl.BlockSpec((c2, d), const),
                  pl.BlockSpec((1, d), const),
                  pl.BlockSpec((1, d), const)],
        out_specs=pl.BlockSpec((tm, d), lambda i: (i, 0)),
        out_shape=jax.ShapeDtypeStruct((t, d), F32),
        compiler_params=_params(("arbitrary",), 48),
        name="outproj_ln",
    )(y_conv, y_ssd, h, w1, w2, g.reshape(1, d), b.reshape(1, d))


def _scores_kernel(h_ref, wq_ref, keys_ref, s_ref, *, nsides, half):
    q = _dot(h_ref[...].astype(BF16), wq_ref[...])
    for hs in range(nsides):
        qh = q[:, hs * half:(hs + 1) * half].astype(BF16)
        s_ref[hs] = _dot_nt(keys_ref[hs], qh)


def _scores_call(h, wq, keys, tm):
    t, d = h.shape
    heads, _, nkeys, half = keys.shape
    qd = wq.shape[1]
    keys2 = keys.reshape(heads * 2, nkeys, half).astype(BF16)
    return pl.pallas_call(
        functools.partial(_scores_kernel, nsides=heads * 2, half=half),
        grid=(t // tm,),
        in_specs=[pl.BlockSpec((tm, d), lambda i: (i, 0)),
                  pl.BlockSpec((d, qd), lambda i: (0, 0)),
                  pl.BlockSpec((heads * 2, nkeys, half), lambda i: (0, 0, 0))],
        out_specs=pl.BlockSpec((heads * 2, nkeys, tm), lambda i: (0, 0, i)),
        out_shape=jax.ShapeDtypeStruct((heads * 2, nkeys, t), F32),
        compiler_params=_params(("arbitrary",), 48),
        name="peer_scores",
    )(h, wq, keys2)


TOPK_HEADS = 2


def _topk_scratch(nkeys):
    k = PEER_TOPK
    return [pltpu.VMEM((nkeys, LANES), F32),
            pltpu.VMEM((5 * k, LANES), F32),
            pltpu.VMEM((2, k, LANES), F32),
            pltpu.VMEM((2, k, LANES), F32),
            pltpu.VMEM((k, LANES), F32),
            pltpu.VMEM((k, LANES), F32)]


def _topk_work(s_ref, idx_ref, gate_ref, scratch, nkeys):
    s_scr, c_scr, ts_scr, ti_scr, bs_scr, bj_scr = scratch
    k = PEER_TOPK
    iota1 = lax.broadcasted_iota(jnp.int32, (nkeys, LANES), 0).astype(F32)
    row8 = lax.broadcasted_iota(jnp.int32, (SUBLANES, LANES), 0)
    flat = [lax.broadcasted_iota(jnp.int32, (k, LANES), 0)]
    flat += [k1 * k + row8 for k1 in range(1, SUBLANES)]
    flat.append((SUBLANES + row8) * k)
    flat = jnp.concatenate(flat, axis=0).astype(F32)

    def select_round(src, ids, nvals, r, out_s, out_i):
        s = src[...]
        m = jnp.max(s, axis=0, keepdims=True)
        idx = jnp.min(jnp.where(s == m, ids, float(nvals)), axis=0, keepdims=True)
        out_s[r:r + 1, :] = m
        out_i[r:r + 1, :] = idx
        src[...] = jnp.where(ids == idx, NEG_INF, s)

    for hd in range(TOPK_HEADS):
        for side in range(2):
            s_scr[...] = s_ref[hd * 2 + side]
            for r in range(k):
                select_round(s_scr, iota1, nkeys, r, ts_scr.at[side], ti_scr.at[side])
                yield
        s0 = ts_scr[0]
        s1 = ts_scr[1]
        cand = [s0[0:1, :] + s1]
        for k1 in range(1, SUBLANES):
            v = s0[k1:k1 + 1, :] + s1[0:SUBLANES, :]
            cand.append(jnp.where(row8 < k // (k1 + 1), v, NEG_INF))
        cand.append(s0[SUBLANES:, :] + s1[0:1, :])
        c_scr[...] = jnp.concatenate(cand, axis=0)
        yield
        for r in range(k):
            select_round(c_scr, flat, k * k, r, bs_scr, bj_scr)
            yield
        best_s = bs_scr[...]
        best_j = bj_scr[...].astype(jnp.int32)
        k1 = best_j // k
        k2 = best_j % k
        i0 = ti_scr[0]
        i1 = ti_scr[1]
        e1 = jnp.zeros((k, LANES), F32)
        e2 = jnp.zeros((k, LANES), F32)
        for kk in range(k):
            e1 = e1 + jnp.where(k1 == kk, i0[kk:kk + 1, :], 0.0)
            e2 = e2 + jnp.where(k2 == kk, i1[kk:kk + 1, :], 0.0)
        ex = jnp.exp(best_s - best_s[0:1, :])
        idx_ref[hd * k:(hd + 1) * k, :] = (e1 * nkeys + e2).astype(jnp.int32)
        gate_ref[hd * k:(hd + 1) * k, :] = ex / jnp.sum(ex, axis=0, keepdims=True)
        yield


def _topk_kernel(s_ref, idx_ref, gate_ref, *scratch, nkeys):
    for _ in _topk_work(s_ref, idx_ref, gate_ref, scratch, nkeys):
        pass


def _topk_specs(heads, nkeys, group0):
    k = PEER_TOPK
    units = heads // TOPK_HEADS
    s_spec = pl.BlockSpec((2 * TOPK_HEADS, nkeys, LANES), lambda j: (j % units, 0, group0 + j // units))
    o_spec = pl.BlockSpec((TOPK_HEADS * k, LANES), lambda j: (j % units, j // units))
    return units, s_spec, o_spec


def _topk_call(scores, token0, ntok):
    nsides, nkeys, _ = scores.shape
    heads = nsides // 2
    k = PEER_TOPK
    assert k == 2 * SUBLANES
    units, s_spec, o_spec = _topk_specs(heads, nkeys, token0 // LANES)
    return pl.pallas_call(
        functools.partial(_topk_kernel, nkeys=nkeys),
        grid=(units * (ntok // LANES),),
        in_specs=[s_spec],
        out_specs=[o_spec, o_spec],
        out_shape=[jax.ShapeDtypeStruct((heads * k, ntok), jnp.int32),
                   jax.ShapeDtypeStruct((heads * k, ntok), F32)],
        scratch_shapes=_topk_scratch(nkeys),
        compiler_params=_params(("arbitrary",), 32),
        name="peer_topk",
    )(scores)


def _gelu(v):
    return 0.5 * v * (1.0 + lax.erf(v * (2.0 ** -0.5)))


def _pack_expert_tables(u_tab, v_tab):
    e, d = u_tab.shape
    half = d // 2

    def pack(tab):
        bits = lax.bitcast_convert_type(tab.astype(BF16), jnp.uint16).astype(jnp.uint32)
        words = bits[:, :half] | (bits[:, half:] << 16)
        return lax.bitcast_convert_type(words, jnp.int32).reshape(e, half // LANES, LANES)

    return jnp.concatenate([pack(u_tab), pack(v_tab)], axis=1)


def _unpack_words(w):
    lo = lax.bitcast_convert_type(w << 16, F32)
    hi = lax.bitcast_convert_type(w & jnp.int32(-65536), F32)
    return lo, hi


ROUTE_PIECES = 8
GATHER_BUFFERS = 4
GATHER_AHEAD = 2


def _gather_kernel(idx_cur_ref, idx_nxt_ref, x_ref, g_ref, acc_ref, *refs, tb, ne, d, route_nkeys):
    del acc_ref
    if route_nkeys:
        s_ref, tab_hbm, o_ref, idx_o_ref, gate_o_ref = refs[:5]
        scratch = refs[5:]
    else:
        tab_hbm, o_ref = refs[:2]
        scratch = refs[2:]
    bufs = scratch[:GATHER_BUFFERS]
    w_scr, sem = scratch[GATHER_BUFFERS:GATHER_BUFFERS + 2]
    side_work = None
    if route_nkeys:
        side_work = _topk_work(s_ref, idx_o_ref, gate_o_ref, scratch[GATHER_BUFFERS + 2:], route_nkeys)
    side_items = TOPK_HEADS * (3 * PEER_TOPK + 2)
    side_points = GATHER_BUFFERS * 2 * tb
    progress = {"points": 0, "items": 0}

    def side_step():
        if side_work is None:
            return
        progress["points"] += 1
        target = -(-side_items * progress["points"] // side_points)
        while progress["items"] < target:
            next(side_work, None)
            progress["items"] += 1

    j = pl.program_id(0)
    nsteps = pl.num_programs(0)
    groups = ne // SUBLANES
    nc = d // (2 * LANES)

    def row_copy(idx_ref, row, t, k, dst):
        e = idx_ref[row, k]
        return pltpu.make_async_copy(tab_hbm.at[e],
                                     bufs[dst].at[t * groups + k // SUBLANES, :, k % SUBLANES, :],
                                     sem.at[dst])

    def block_wait(src):
        pltpu.make_async_copy(bufs[src], bufs[src], sem.at[src]).wait()

    @pl.when(j == 0)
    def _():
        def body(t, carry):
            for b in range(GATHER_AHEAD):
                for k in range(ne):
                    row_copy(idx_cur_ref, b * tb + t, t, k, b).start()
            return carry
        lax.fori_loop(0, tb, body, 0)

    def block(par):
        buf = bufs[par]
        block_wait(par)
        ahead = par + GATHER_AHEAD
        nxt_ref = idx_cur_ref if ahead < GATHER_BUFFERS else idx_nxt_ref
        nxt_row0 = (ahead % GATHER_BUFFERS) * tb
        dst = ahead % GATHER_BUFFERS
        fetches = [(t, k) for t in range(tb) for k in range(ne)]
        per_phase = len(fetches) // (2 * tb)

        def fetch_some(phase):
            for t, k in fetches[phase * per_phase:(phase + 1) * per_phase]:
                row_copy(nxt_ref, nxt_row0 + t, t, k, dst).start(priority=k % 2)

        cols = []
        for t in range(tb):
            x = x_ref[par * tb + t:par * tb + t + 1, :]
            xs = [x[:, c * LANES:(c + 1) * LANES] for c in range(2 * nc)]
            parts = []
            for g in range(groups):
                acc = None
                for c in range(nc):
                    lo, hi = _unpack_words(buf[t * groups + g, c])
                    term = lo * xs[c] + hi * xs[nc + c]
                    acc = term if acc is None else acc + term
                parts.append(acc)
            cols.append(jnp.sum(jnp.concatenate(parts, axis=0), axis=-1, keepdims=True))
            fetch_some(t)
            side_step()
        act = jnp.concatenate(cols, axis=1)
        w_scr[par] = g_ref[par] * _gelu(act)
        for t in range(tb):
            acc_lo = [None] * nc
            acc_hi = [None] * nc
            for g in range(groups):
                wb = jnp.broadcast_to(w_scr[par, g * SUBLANES:(g + 1) * SUBLANES, t:t + 1],
                                      (SUBLANES, LANES))
                for c in range(nc):
                    lo, hi = _unpack_words(buf[t * groups + g, nc + c])
                    acc_lo[c] = lo * wb if acc_lo[c] is None else acc_lo[c] + lo * wb
                    acc_hi[c] = hi * wb if acc_hi[c] is None else acc_hi[c] + hi * wb
            o_ref[par * tb + t:par * tb + t + 1, :] = jnp.concatenate(
                [jnp.sum(a, axis=0, keepdims=True) for a in acc_lo + acc_hi], axis=1)
            fetch_some(tb + t)
            side_step()

    for par in range(GATHER_BUFFERS):
        block(par)
    if side_work is not None:
        for _ in side_work:
            pass

    @pl.when(j == nsteps - 1)
    def _():
        for b in range(GATHER_AHEAD):
            block_wait(b)


def _gather_call(h, idx, gate3, table, tb, ffn_acc, token0=0, scores=None, route_token0=0):
    t, d = h.shape
    ntok, ne = idx.shape
    rows = GATHER_BUFFERS * tb
    ns = ntok // rows
    step0 = token0 // rows
    nrows = table.shape[1]
    in_specs = [pl.BlockSpec((rows, ne), lambda j: (j, 0), memory_space=pltpu.SMEM),
                pl.BlockSpec((rows, ne), lambda j: (jnp.minimum(j + 1, ns - 1), 0),
                             memory_space=pltpu.SMEM),
                pl.BlockSpec((rows, d), lambda j: (step0 + j, 0)),
                pl.BlockSpec((GATHER_BUFFERS, ne, tb), lambda j: (j, 0, 0)),
                pl.BlockSpec(memory_space=pl.ANY)]
    out_specs = [pl.BlockSpec((rows, d), lambda j: (step0 + j, 0))]
    out_shape = [jax.ShapeDtypeStruct((t, d), F32)]
    scratch = [pltpu.VMEM((tb * ne // SUBLANES, nrows, SUBLANES, LANES), jnp.int32)
               for _ in range(GATHER_BUFFERS)]
    scratch += [pltpu.VMEM((GATHER_BUFFERS, ne, tb), F32), pltpu.SemaphoreType.DMA((GATHER_BUFFERS,))]
    args = [idx, idx, h, gate3, ffn_acc]
    route_nkeys = 0
    if scores is not None:
        nsides, route_nkeys, _ = scores.shape
        units, s_spec, o_spec = _topk_specs(nsides // 2, route_nkeys, route_token0 // LANES)
        assert ns == units * (ntok // LANES)
        in_specs.append(s_spec)
        out_specs += [o_spec, o_spec]
        out_shape += [jax.ShapeDtypeStruct((ne, ntok), jnp.int32), jax.ShapeDtypeStruct((ne, ntok), F32)]
        scratch += _topk_scratch(route_nkeys)
        args.append(scores)
    in_specs.append(pl.BlockSpec(memory_space=pl.ANY))
    args.append(table)
    out = pl.pallas_call(
        functools.partial(_gather_kernel, tb=tb, ne=ne, d=d, route_nkeys=route_nkeys),
        grid=(ns,),
        in_specs=in_specs,
        out_specs=out_specs,
        out_shape=out_shape,
        scratch_shapes=scratch,
        input_output_aliases={4: 0},
        compiler_params=_params(("arbitrary",), 56),
        name="peer_gather_route" if route_nkeys else "peer_gather",
    )(*args)
    return out if route_nkeys else out[0]


def _ple_kernel(h_ref, f_ref, p_ref, wg_ref, wp_ref, g_ref, b_ref, o_ref, *, alpha):
    h = h_ref[...]
    gate = _sigmoid(_dot(h.astype(BF16), wg_ref[...]))
    emb = _dot(p_ref[...].astype(BF16), wp_ref[...])
    o_ref[...] = _layer_norm(alpha * h + f_ref[...] + gate * emb, g_ref[...], b_ref[...])


def _ple_call(h, ffn, p, wg, wp, g, b, alpha, tm):
    t, d = h.shape
    pd = p.shape[1]
    const = lambda i: (0, 0)
    return pl.pallas_call(
        functools.partial(_ple_kernel, alpha=alpha),
        grid=(t // tm,),
        in_specs=[pl.BlockSpec((tm, d), lambda i: (i, 0)),
                  pl.BlockSpec((tm, d), lambda i: (i, 0)),
                  pl.BlockSpec((tm, pd), lambda i: (i, 0)),
                  pl.BlockSpec((d, d), const),
                  pl.BlockSpec((pd, d), const),
                  pl.BlockSpec((1, d), const),
                  pl.BlockSpec((1, d), const)],
        out_specs=pl.BlockSpec((tm, d), lambda i: (i, 0)),
        out_shape=jax.ShapeDtypeStruct((t, d), F32),
        compiler_params=_params(("arbitrary",), 48),
        name="ple_ln",
    )(h, ffn, p, wg, wp, g.reshape(1, d), b.reshape(1, d))


def _tile(n, want):
    return want if n % want == 0 else n


def kernel(x, p, ln_in_g, ln_in_b, w_in, conv_w, conv_b, conv_ln_g, conv_ln_b, ssd_conv_w, ssd_conv_b, dt_bias, a_log, d_skip, ssd_norm_g, w_out, ln1_g, ln1_b, peer_wq, peer_keys, peer_u, peer_v, ple_w_gate, ple_w_proj, ln2_g, ln2_b):
    batch, seq, d = x.shape
    depth = w_in.shape[0]
    t = batch * seq
    conv_width = conv_w.shape[2]
    width = ssd_norm_g.shape[1]
    cx = ssd_conv_w.shape[2]
    heads = dt_bias.shape[1]
    n_main = 2 * conv_width + width + cx
    alpha = float((2 * depth) ** 0.25)
    q = SSD_CHUNK

    tm = _tile(t, 256)
    ts = _tile(seq, 256)
    tb = 8
    tn_main = n_main // 3 if (n_main // 3) % LANES == 0 and n_main % 3 == 0 else n_main

    h = _ln_call(x.reshape(t, d), ln_in_g, ln_in_b, tm)
    p2 = p.reshape(depth, t, p.shape[-1])
    ffn = jnp.zeros((t, d), F32)
    for i in range(depth):
        w_main = w_in[i, :, :n_main].astype(BF16)
        w_dt = jnp.pad(w_in[i, :, n_main:], ((0, 0), (0, LANES - heads))).astype(BF16)
        proj = _matmul_call(h, w_main, _tile(t, 512), tn_main, "in_proj")
        dt_raw = _matmul_call(h, w_dt, _tile(t, 512), LANES, "dt_proj")
        dtp = (dt_raw[:, :heads].reshape(t // q, q, heads // 2, 2)
               .transpose(2, 0, 3, 1).reshape(heads // 2, 2 * t))
        y_conv = _conv_call(proj, conv_w[i], conv_b[i], conv_ln_g[i], conv_ln_b[i], batch, seq, ts)
        y_ssd = _ssd_call(proj, dt_raw, dtp, ssd_conv_w[i], ssd_conv_b[i], dt_bias[i], a_log[i],
                          d_skip[i], ssd_norm_g[i], batch, seq, ts, conv_width)
        w_o = w_out[i].astype(BF16)
        h = _outproj_call(y_conv, y_ssd, h, w_o[:conv_width], w_o[conv_width:], ln1_g[i], ln1_b[i],
                          alpha, tm)
        scores = _scores_call(h, peer_wq[i].astype(BF16), peer_keys[i], tm)
        table = _pack_expert_tables(peer_u[i], peer_v[i])
        tp = t // ROUTE_PIECES
        idx_t, gate_t = _topk_call(scores, 0, tp)
        for piece in range(ROUTE_PIECES):
            ne = idx_t.shape[0]
            idx = idx_t.T
            gate3 = gate_t.reshape(ne, tp // tb, tb).transpose(1, 0, 2)
            if piece + 1 < ROUTE_PIECES:
                ffn, idx_t, gate_t = _gather_call(h, idx, gate3, table, tb, ffn, piece * tp,
                                                  scores, (piece + 1) * tp)
            else:
                ffn = _gather_call(h, idx, gate3, table, tb, ffn, piece * tp)
        h = _ple_call(h, ffn, p2[i], ple_w_gate[i].astype(BF16), ple_w_proj[i].astype(BF16),
                      ln2_g[i], ln2_b[i], alpha, tm)
    return h.reshape(batch, seq, d)
```
